```python
import jax, jax.numpy as jnp
from jax import lax
import numpy as np

D_MODEL = 1024
BATCH = 32
SEQ = 256
DEPTH = 2
DEC_BATCH = 4
DEC_SEQ = 1024
PAST_LEN = 256

GRID_W = 64
MLA_HEADS = 8
Q_LORA = 384
KV_LORA = 256
QK_NOPE = 64
QK_ROPE = 32
V_HEAD = 64
MLA_W = MLA_HEADS * V_HEAD
SOFTMAX_SCALE = (QK_NOPE + QK_ROPE) ** -0.5
ROPE_BASE = 10000.0
Q_BLOCK = 128
CONV_W = 256
CONV_K = 31
GMLP_W = 256
GMLP_GROUPS = 4
GMLP_CHUNK = 128
POOL_W = 256
POOL_WINDOWS = (2, 4, 8, 16)
POOL_GROUPS = 4
N_BRANCH = 4
D_FF = 2816
FFN_K = 3
NORM_EPS = 1e-6
OFF_KV = Q_LORA
OFF_CONV = OFF_KV + KV_LORA + QK_ROPE
OFF_GMLP = OFF_CONV + 2 * CONV_W
OFF_POOL = OFF_GMLP + 2 * GMLP_W
OFF_GATE = OFF_POOL + POOL_W
D_IN = OFF_GATE + N_BRANCH * D_MODEL

kernel_name = "hybrid_diffusion_parallel_mla_conv_gmlp_pool_step"


def rmsnorm(x, g):
    xf = x.astype(jnp.float32)
    y = xf * lax.rsqrt(jnp.mean(xf * xf, axis=-1, keepdims=True) + NORM_EPS)
    return (y * g.astype(jnp.float32)).astype(x.dtype)


def layernorm(x, g, b):
    xf = x.astype(jnp.float32)
    mu = jnp.mean(xf, axis=-1, keepdims=True)
    var = jnp.mean(jnp.square(xf - mu), axis=-1, keepdims=True)
    y = (xf - mu) * lax.rsqrt(var + NORM_EPS)
    return (y * g.astype(jnp.float32) + b.astype(jnp.float32)).astype(x.dtype)


def depthwise_conv(x, w, b):
    C = x.shape[-1]
    y = lax.conv_general_dilated(x, w[:, None, :].astype(x.dtype), window_strides=(1,), padding='SAME',
                                 dimension_numbers=('NWC', 'WIO', 'NWC'), feature_group_count=C)
    return y + b


def grid_positions(L):
    rows = L // GRID_W
    row = jnp.repeat(jnp.arange(rows), GRID_W)
    col = jnp.tile(jnp.arange(GRID_W), rows)
    return row, col


def rope_1d(x, pos):
    half = x.shape[-1] // 2
    inv = ROPE_BASE ** (-jnp.arange(half, dtype=jnp.float32) / half)
    ang = pos.astype(jnp.float32)[:, None] * inv[None, :]
    cos = jnp.cos(ang)[None, :, None, :].astype(x.dtype)
    sin = jnp.sin(ang)[None, :, None, :].astype(x.dtype)
    x1, x2 = x[..., :half], x[..., half:]
    return jnp.concatenate([x1 * cos - x2 * sin, x1 * sin + x2 * cos], axis=-1)


def axial_rope(x, row, col):
    r = x.shape[-1] // 2
    return jnp.concatenate([rope_1d(x[..., :r], row), rope_1d(x[..., r:], col)], axis=-1)


def kv_expand(c_kv, w_kv_b):
    B, L, _ = c_kv.shape
    kv = (c_kv @ w_kv_b).reshape(B, L, MLA_HEADS, QK_NOPE + V_HEAD)
    return kv[..., :QK_NOPE], kv[..., QK_NOPE:]


def blocked_attention(q_nope, q_pe_parts, k_nope, k_pe_parts, v):
    B, L, H, _ = q_nope.shape
    nb = L // Q_BLOCK

    def to_blocks(t):
        return jnp.moveaxis(t.reshape((B, nb, Q_BLOCK) + t.shape[2:]), 1, 0)

    def one_block(qs):
        qn, qpes = qs
        s_pe = jnp.concatenate([jnp.einsum('bqhr,bkr->bhqk', qp, kp) for qp, kp in zip(qpes, k_pe_parts)], axis=-1)
        s = jnp.einsum('bqhd,bkhd->bhqk', qn, k_nope) + s_pe
        p = jax.nn.softmax(s.astype(jnp.float32), axis=-1).astype(v.dtype)
        return jnp.einsum('bhqk,bkhd->bqhd', p, v)

    out = lax.map(one_block, (to_blocks(q_nope), tuple(to_blocks(t) for t in q_pe_parts)))
    return jnp.moveaxis(out, 0, 1).reshape(B, L, H, v.shape[-1])


def centred_mean_minus_self(x, w):
    L = x.shape[1]
    lo = w // 2
    hi = w - lo - 1
    xf = x.astype(jnp.float32)
    cs = jnp.concatenate([jnp.zeros_like(xf[:, :1]), jnp.cumsum(xf, axis=1)], axis=1)
    t = jnp.arange(L)
    end = jnp.minimum(t + hi + 1, L)
    start = jnp.maximum(t - lo, 0)
    cnt = (end - start).astype(jnp.float32)[None, :, None]
    mean = (jnp.take(cs, end, axis=1) - jnp.take(cs, start, axis=1)) / cnt
    return (mean - xf).astype(x.dtype)


def token_mixing(h, lp, row, col, ctx):
    B, L, _ = h.shape
    proj = h @ lp['w_in']
    q_a = proj[..., :OFF_KV]
    kv_a = proj[..., OFF_KV:OFF_CONV]
    conv_in = proj[..., OFF_CONV:OFF_GMLP]
    gmlp_in = proj[..., OFF_GMLP:OFF_POOL]
    pool_in = proj[..., OFF_POOL:OFF_GATE]
    gate_logits = proj[..., OFF_GATE:]

    q = (rmsnorm(q_a, lp['q_norm_g']) @ lp['w_q_b']).reshape(B, L, MLA_HEADS, QK_NOPE + QK_ROPE) * SOFTMAX_SCALE
    q_nope, q_pe = q[..., :QK_NOPE], q[..., QK_NOPE:]
    c_kv = rmsnorm(kv_a[..., :KV_LORA], lp['kv_norm_g'])
    k_pe = kv_a[..., KV_LORA:]
    kn, vv = kv_expand(c_kv, lp['w_kv_b'])
    if ctx is None:
        attn = blocked_attention(q_nope, (q_pe,), kn, (k_pe,), vv)
    else:
        ctx_ckv, ctx_kpe = ctx
        kn_c, v_c = kv_expand(ctx_ckv, lp['w_kv_b'])
        q_pe_rot = axial_rope(q_pe, row, col)
        k_pe_rot = axial_rope(k_pe[:, :, None, :], row, col)[:, :, 0, :]
        attn = blocked_attention(q_nope, (q_pe_rot, q_pe), jnp.concatenate([kn, kn_c], axis=1),
                                 (k_pe_rot, ctx_kpe), jnp.concatenate([vv, v_c], axis=1))
    branch_a = attn.reshape(B, L, MLA_W) @ lp['w_o_mla']

    a = conv_in[..., :CONV_W] * jax.nn.sigmoid(conv_in[..., CONV_W:])
    a = depthwise_conv(a, lp['conv_dw'], lp['conv_dw_b'])
    a = jax.nn.silu(layernorm(a, lp['conv_ln_g'], lp['conv_ln_b']))
    branch_b = a @ lp['w_conv_out']

    uv = jax.nn.gelu(gmlp_in, approximate=False)
    u, v = uv[..., :GMLP_W], uv[..., GMLP_W:]
    v = layernorm(v, lp['gmlp_ln_g'], lp['gmlp_ln_b'])
    vc = v.reshape(B, L // GMLP_CHUNK, GMLP_CHUNK, GMLP_GROUPS, GMLP_W // GMLP_GROUPS)
    sv = jnp.einsum('gpq,bnqgc->bnpgc', lp['gmlp_ws'], vc) + lp['gmlp_bs'].T[:, :, None]
    branch_c = (u * sv.reshape(B, L, GMLP_W)) @ lp['w_gmlp_out']

    groups = jnp.split(pool_in, POOL_GROUPS, axis=-1)
    pooled = jnp.stack([centred_mean_minus_self(g, w) for g, w in zip(groups, POOL_WINDOWS)], axis=2)
    pmix = jnp.einsum('blgc,gcd->blgd', pooled, lp['pool_w']).reshape(B, L, POOL_W) * lp['pool_scale']
    branch_d = pmix @ lp['w_pool_out']

    gates = jax.nn.sigmoid(gate_logits + lp['b_gate']).reshape(B, L, N_BRANCH, D_MODEL)
    merged = (gates[:, :, 0] * branch_a + gates[:, :, 1] * branch_b
              + gates[:, :, 2] * branch_c + gates[:, :, 3] * branch_d)
    return merged @ lp['w_out'], c_kv, k_pe


def conv_ffn(h, lp):
    up = depthwise_conv(h @ lp['ffn_up'], lp['ffn_dw'], lp['ffn_dw_b'])
    gate, val = up[..., :D_FF], up[..., D_FF:]
    return (jax.nn.silu(gate) * val) @ lp['ffn_down']


def trunk_layer(x, mod, lp, row, col, ctx):
    sh1, sc1, g1, sh2, sc2, g2 = jnp.split(mod[:, None, :], 6, axis=-1)
    h = rmsnorm(x, lp['pre_mix_g']) * (1.0 + sc1) + sh1
    mix, c_kv, k_pe = token_mixing(h, lp, row, col, ctx)
    x = x + g1 * rmsnorm(mix, lp['post_mix_g'])
    h = rmsnorm(x, lp['pre_ffn_g']) * (1.0 + sc2) + sh2
    x = x + g2 * rmsnorm(conv_ffn(h, lp), lp['post_ffn_g'])
    return x, c_kv, k_pe


def setup_inputs(seed: int = 0) -> dict:
    key = jax.random.key(seed)
    ks = iter(jax.random.split(key, 48))

    def nrm(shape, scale):
        return jax.random.normal(next(ks), shape, jnp.float32) * scale

    def gain(shape):
        return 1.0 + nrm(shape, 0.05)

    L = DEPTH
    return {
        'x_prompt': nrm((BATCH, SEQ, D_MODEL), 1.0),
        'x_sample': nrm((DEC_BATCH, DEC_SEQ, D_MODEL), 1.0),
        'cache_ckv': nrm((DEC_BATCH, DEPTH, PAST_LEN, KV_LORA), 1.0),
        'cache_kpe': nrm((DEC_BATCH, DEPTH, PAST_LEN, QK_ROPE), 1.0),
        'c': nrm((DEC_BATCH, D_MODEL), 1.0),
        'c_ctx': nrm((D_MODEL,), 1.0),
        'ada_w': nrm((L, D_MODEL, 6 * D_MODEL), D_MODEL ** -0.5),
        'ada_b': nrm((L, 6 * D_MODEL), 0.02),
        'pre_mix_g': gain((L, D_MODEL)),
        'post_mix_g': gain((L, D_MODEL)),
        'pre_ffn_g': gain((L, D_MODEL)),
        'post_ffn_g': gain((L, D_MODEL)),
        'w_in': nrm((L, D_MODEL, D_IN), D_MODEL ** -0.5),
        'b_gate': nrm((L, N_BRANCH * D_MODEL), 0.1),
        'q_norm_g': gain((L, Q_LORA)),
        'w_q_b': nrm((L, Q_LORA, MLA_HEADS * (QK_NOPE + QK_ROPE)), Q_LORA ** -0.5),
        'kv_norm_g': gain((L, KV_LORA)),
        'w_kv_b': nrm((L, KV_LORA, MLA_HEADS * (QK_NOPE + V_HEAD)), KV_LORA ** -0.5),
        'w_o_mla': nrm((L, MLA_W, D_MODEL), MLA_W ** -0.5),
        'conv_dw': nrm((L, CONV_K, CONV_W), CONV_K ** -0.5),
        'conv_dw_b': nrm((L, CONV_W), 0.02),
        'conv_ln_g': gain((L, CONV_W)),
        'conv_ln_b': nrm((L, CONV_W), 0.02),
        'w_conv_out': nrm((L, CONV_W, D_MODEL), CONV_W ** -0.5),
        'gmlp_ln_g': gain((L, GMLP_W)),
        'gmlp_ln_b': nrm((L, GMLP_W), 0.02),
        'gmlp_ws': nrm((L, GMLP_GROUPS, GMLP_CHUNK, GMLP_CHUNK), GMLP_CHUNK ** -0.5),
        'gmlp_bs': 1.0 + nrm((L, GMLP_GROUPS, GMLP_CHUNK), 0.1),
        'w_gmlp_out': nrm((L, GMLP_W, D_MODEL), GMLP_W ** -0.5),
        'pool_w': nrm((L, POOL_GROUPS, POOL_W // POOL_GROUPS, POOL_W // POOL_GROUPS), (POOL_W // POOL_GROUPS) ** -0.5),
        'pool_scale': 1.0 + nrm((L, POOL_W), 0.1),
        'w_pool_out': nrm((L, POOL_W, D_MODEL), POOL_W ** -0.5),
        'w_out': nrm((L, D_MODEL, D_MODEL), D_MODEL ** -0.5),
        'ffn_up': nrm((L, D_MODEL, 2 * D_FF), D_MODEL ** -0.5),
        'ffn_dw': nrm((L, FFN_K, 2 * D_FF), FFN_K ** -0.5),
        'ffn_dw_b': nrm((L, 2 * D_FF), 0.02),
        'ffn_down': nrm((L, D_FF, D_MODEL), D_FF ** -0.5),
    }


def reference(x_prompt, x_sample, cache_ckv, cache_kpe, c, c_ctx, ada_w, ada_b, pre_mix_g, post_mix_g,
              pre_ffn_g, post_ffn_g, w_in, b_gate, q_norm_g, w_q_b, kv_norm_g, w_kv_b, w_o_mla,
              conv_dw, conv_dw_b, conv_ln_g, conv_ln_b, w_conv_out, gmlp_ln_g, gmlp_ln_b, gmlp_ws,
              gmlp_bs, w_gmlp_out, pool_w, pool_scale, w_pool_out, w_out, ffn_up, ffn_dw, ffn_dw_b,
              ffn_down):
    weights = dict(pre_mix_g=pre_mix_g, post_mix_g=post_mix_g, pre_ffn_g=pre_ffn_g, post_ffn_g=post_ffn_g,
                   w_in=w_in, b_gate=b_gate, q_norm_g=q_norm_g, w_q_b=w_q_b, kv_norm_g=kv_norm_g,
                   w_kv_b=w_kv_b, w_o_mla=w_o_mla, conv_dw=conv_dw, conv_dw_b=conv_dw_b,
                   conv_ln_g=conv_ln_g, conv_ln_b=conv_ln_b, w_conv_out=w_conv_out, gmlp_ln_g=gmlp_ln_g,
                   gmlp_ln_b=gmlp_ln_b, gmlp_ws=gmlp_ws, gmlp_bs=gmlp_bs, w_gmlp_out=w_gmlp_out,
                   pool_w=pool_w, pool_scale=pool_scale, w_pool_out=w_pool_out, w_out=w_out,
                   ffn_up=ffn_up, ffn_dw=ffn_dw, ffn_dw_b=ffn_dw_b, ffn_down=ffn_down)
    row, col = grid_positions(x_sample.shape[1])
    xp = x_prompt
    xs = x_sample
    ckv_list = []
    kpe_list = []
    for l in range(DEPTH):
        lp = {k: v[l] for k, v in weights.items()}
        mod_ctx = jax.nn.silu(c_ctx)[None, :] @ ada_w[l] + ada_b[l]
        xp, ckv_l, kpe_l = trunk_layer(xp, mod_ctx, lp, None, None, None)
        ckv_list.append(ckv_l)
        kpe_list.append(kpe_l)
        mod_lat = jax.nn.silu(c) @ ada_w[l] + ada_b[l]
        xs, _, _ = trunk_layer(xs, mod_lat, lp, row, col, (cache_ckv[:, l], cache_kpe[:, l]))
    new_ckv = jnp.stack(ckv_list, axis=1)
    new_kpe = jnp.stack(kpe_list, axis=1)
    return (xp, xs, new_ckv, new_kpe)
```

```python
import functools

import jax
import jax.numpy as jnp
import numpy as np
from jax import lax
from jax.experimental import pallas as pl
from jax.experimental.pallas import tpu as pltpu

D_MODEL = 1024
DEPTH = 2
GRID_W = 64
MLA_HEADS = 8
Q_LORA = 384
KV_LORA = 256
QK_NOPE = 64
QK_ROPE = 32
V_HEAD = 64
MLA_W = MLA_HEADS * V_HEAD
SOFTMAX_SCALE = (QK_NOPE + QK_ROPE) ** -0.5
ROPE_BASE = 10000.0
CONV_W = 256
CONV_K = 31
GMLP_W = 256
GMLP_GROUPS = 4
GMLP_CHUNK = 128
POOL_W = 256
POOL_GROUPS = 4
N_BRANCH = 4
D_FF = 2816
FFN_K = 3
NORM_EPS = 1e-6
OFF_KV = Q_LORA
OFF_KPE = OFF_KV + KV_LORA
OFF_CONV = OFF_KPE + QK_ROPE
OFF_GMLP = OFF_CONV + 2 * CONV_W
OFF_POOL = OFF_GMLP + 2 * GMLP_W
OFF_GATE = OFF_POOL + POOL_W

LANES = 128
SUBLANES = 8
VMEM_LIMIT_BYTES = 56 * 1024 * 1024

S_Q = 0
S_KV = S_Q + Q_LORA
S_CONV = S_KV + KV_LORA
S_GMLP = S_CONV + 2 * CONV_W
S_POOL = S_GMLP + 2 * GMLP_W
S_KPE = S_POOL + POOL_W
S_KPEP = S_KPE + LANES
S_TOTAL = S_KPEP + LANES

HEAD_SLOT = LANES
QK_CAT = MLA_HEADS * HEAD_SLOT

ROW_TILE = 512
FFN_TILE = 1024
FFN_CHUNK = 256
FFN_NCHUNK = D_FF // FFN_CHUNK
SEQ_ROWS = 64
PAD_ROWS = 16
ATTN_Q_TILE = 256

BF16 = jnp.bfloat16
F32 = jnp.float32
INV_SQRT2 = 0.7071067811865476


def _dot(a, b):
    return jnp.dot(a, b, preferred_element_type=F32)


def _dot_nt(a, b):
    return lax.dot_general(a, b, (((1,), (1,)), ((), ())), preferred_element_type=F32)


def _rms(x, g):
    return x * lax.rsqrt(jnp.mean(x * x, axis=-1, keepdims=True) + NORM_EPS) * g


def _layernorm(x, g, b):
    mu = jnp.mean(x, axis=-1, keepdims=True)
    xc = x - mu
    var = jnp.mean(xc * xc, axis=-1, keepdims=True)
    return xc * lax.rsqrt(var + NORM_EPS) * g + b


def _sigmoid(x):
    return 1.0 / (1.0 + jnp.exp(-x))


def _const_spec(shape):
    nd = len(shape)
    return pl.BlockSpec(shape, lambda *_: (0,) * nd, pipeline_mode=pl.Buffered(1))


def _params(n_axes):
    return pltpu.CompilerParams(dimension_semantics=("arbitrary",) * n_axes,
                                vmem_limit_bytes=VMEM_LIMIT_BYTES)


def _mod_kernel(cv_ref, w_ref, b_ref, o_ref):
    cv = cv_ref[...]
    s = (cv * _sigmoid(cv)).astype(BF16)
    o_ref[0] = _dot(s, w_ref[0].astype(BF16)) + b_ref[0]


def _mod_call(cvec, ada_w, ada_b):
    tn = 1536
    n = 6 * D_MODEL
    return pl.pallas_call(
        _mod_kernel,
        grid=(DEPTH, n // tn),
        in_specs=[pl.BlockSpec((SUBLANES, D_MODEL), lambda l, j: (0, 0)),
                  pl.BlockSpec((1, D_MODEL, tn), lambda l, j: (l, 0, j)),
                  pl.BlockSpec((1, 1, tn), lambda l, j: (l, 0, j))],
        out_specs=pl.BlockSpec((1, SUBLANES, tn), lambda l, j: (l, 0, j)),
        out_shape=jax.ShapeDtypeStruct((DEPTH, SUBLANES, n), F32),
        compiler_params=_params(2),
        name="mod",
    )(cvec, ada_w, ada_b.reshape(DEPTH, 1, n))


def _rowproj_kernel(rope, x_ref, mod_ref, preg_ref, ws_ref, qng_ref, wq_ref, kvng_ref, wkv_ref, e_ref,
                    glng_ref, glnb_ref, gws_ref, gbm_ref, *rest):
    if rope:
        ccat_ref, scat_ref, ckpe_ref, skpe_ref = rest[:4]
        rest = rest[4:]
        h_ref, q_ref, qp_ref, k_ref, v_ref, ckv_ref, kpe_ref, a_ref, pin_ref, fc_ref = rest
    else:
        h_ref, q_ref, k_ref, v_ref, ckv_ref, kpe_ref, a_ref, pin_ref, fc_ref = rest
    x = x_ref[...]
    tm = x.shape[0]
    mod = mod_ref[...]
    sh1 = mod[:, 0:D_MODEL]
    sc1 = mod[:, D_MODEL:2 * D_MODEL]
    h = _rms(x, preg_ref[...]) * (1.0 + sc1) + sh1
    hb = h.astype(BF16)
    h_ref[...] = hb

    qn = _rms(_dot(hb, ws_ref[:, S_Q:S_KV]), qng_ref[...]).astype(BF16)
    qq = _dot(qn, wq_ref[...]) * SOFTMAX_SCALE
    if rope:
        ccat = ccat_ref[...]
        scat = scat_ref[...]
        for hd in range(MLA_HEADS):
            lo = hd * HEAD_SLOT
            qpl = qq[:, lo:lo + HEAD_SLOT]
            qpr = qq[:, QK_CAT + lo:QK_CAT + lo + HEAD_SLOT]
            q_ref[:, lo:lo + HEAD_SLOT] = (qpl * ccat + qpr * scat).astype(BF16)
            qp_ref[:, lo:lo + HEAD_SLOT] = qpl.astype(BF16)
    else:
        q_ref[...] = qq.astype(BF16)
    ckv = _rms(_dot(hb, ws_ref[:, S_KV:S_CONV]), kvng_ref[...])
    ckv_ref[...] = ckv
    kvb = _dot(ckv.astype(BF16), wkv_ref[...])
    kpa = _dot(hb, ws_ref[:, S_KPE:S_KPEP])
    kpe_ref[...] = kpa[:, 0:QK_ROPE]
    if rope:
        kpr = kpa * ckpe_ref[...] + _dot(hb, ws_ref[:, S_KPEP:S_TOTAL]) * skpe_ref[...]
    else:
        kpr = kpa
    k_ref[...] = (kvb[:, 0:QK_CAT] + _dot(kpr.astype(BF16), e_ref[...])).astype(BF16)
    v_ref[...] = kvb[:, QK_CAT:QK_CAT + MLA_W].astype(BF16)

    ci = _dot(hb, ws_ref[:, S_CONV:S_GMLP])
    a_ref[...] = ci[:, 0:CONV_W] * _sigmoid(ci[:, CONV_W:2 * CONV_W])
    pin_ref[...] = _dot(hb, ws_ref[:, S_POOL:S_KPE])

    gm = _dot(hb, ws_ref[:, S_GMLP:S_POOL])
    uv = 0.5 * gm * (1.0 + lax.erf(gm * INV_SQRT2))
    u = uv[:, 0:GMLP_W]
    v = _layernorm(uv[:, GMLP_W:2 * GMLP_W], glng_ref[...], glnb_ref[...])
    grp = lax.broadcasted_iota(jnp.int32, (GMLP_CHUNK, GMLP_W), 1) // (GMLP_W // GMLP_GROUPS)
    gws = gws_ref[...]
    gbm = gbm_ref[...]
    for n in range(tm // GMLP_CHUNK):
        r0 = n * GMLP_CHUNK
        vc = v[r0:r0 + GMLP_CHUNK]
        vs = jnp.concatenate([jnp.where(grp == g, vc, 0.0) for g in range(GMLP_GROUPS)], axis=0)
        sv = _dot(gws, vs.astype(BF16)) + gbm
        fc_ref[r0:r0 + GMLP_CHUNK, :] = (u[r0:r0 + GMLP_CHUNK] * sv).astype(BF16)


def _rowproj_call(x2d, mod3, seq_len, mod_row_of_tile, lw, rope_tabs):
    rows = x2d.shape[0]
    tm = ROW_TILE
    rope = rope_tabs is not None
    wq = lw["wq_rope"] if rope else lw["wq"]
    row_spec = lambda w: pl.BlockSpec((tm, w), lambda i: (i, 0))
    in_specs = [row_spec(D_MODEL),
                pl.BlockSpec((None, 1, 6 * D_MODEL), lambda i: (mod_row_of_tile(i), 0, 0)),
                _const_spec((1, D_MODEL)),
                _const_spec((D_MODEL, S_TOTAL)),
                _const_spec((1, Q_LORA)),
                _const_spec(wq.shape),
                _const_spec((1, KV_LORA)),
                _const_spec(lw["wkv"].shape),
                _const_spec((LANES, QK_CAT)),
                _const_spec((1, GMLP_W)), _const_spec((1, GMLP_W)),
                _const_spec((GMLP_CHUNK, GMLP_GROUPS * GMLP_CHUNK)),
                _const_spec((GMLP_CHUNK, GMLP_W))]
    args = [x2d, mod3, lw["pre_mix_g"], lw["w_small"], lw["q_norm_g"], wq, lw["kv_norm_g"], lw["wkv"],
            lw["e_kpe"], lw["gmlp_ln_g"], lw["gmlp_ln_b"], lw["gmlp_wcat"], lw["gmlp_bmat"]]
    outs = [("h", D_MODEL, BF16), ("q", QK_CAT, BF16)]
    if rope:
        tiles_per_seq = seq_len // tm
        tab_spec = pl.BlockSpec((tm, LANES), lambda i: (i % tiles_per_seq, 0))
        in_specs += [tab_spec] * 4
        args += list(rope_tabs)
        outs.append(("qp", QK_CAT, BF16))
    outs += [("k", QK_CAT, BF16), ("v", MLA_W, BF16), ("ckv", KV_LORA, F32), ("kpe", QK_ROPE, F32),
             ("a", CONV_W, F32), ("pin", POOL_W, F32), ("fc", GMLP_W, BF16)]
    res = pl.pallas_call(
        functools.partial(_rowproj_kernel, rope),
        grid=(rows // tm,),
        in_specs=in_specs,
        out_specs=[row_spec(w) for _, w, _ in outs],
        out_shape=[jax.ShapeDtypeStruct((rows, w), dt) for _, w, dt in outs],
        compiler_params=_params(1),
        name="rowproj_rope" if rope else "rowproj",
    )(*args)
    return {name: r for (name, _, _), r in zip(outs, res)}


def _seqmix_kernel(seq_len, a_ref, pin_ref, cw_ref, cb_ref, clng_ref, clnb_ref, pw_ref, ps_ref,
                   fb_ref, fd_ref, apad_ref, ppad_ref, pooled_ref):
    L = seq_len
    zeros = jnp.zeros((PAD_ROWS, CONV_W), F32)
    for pad_ref, src_ref in ((apad_ref, a_ref), (ppad_ref, pin_ref)):
        pad_ref[0:PAD_ROWS, :] = zeros
        pad_ref[L + PAD_ROWS:L + 2 * PAD_ROWS, :] = zeros
        pad_ref[PAD_ROWS:L + PAD_ROWS, :] = src_ref[...]
    cb = cb_ref[...]
    clng = clng_ref[...]
    clnb = clnb_ref[...]
    half_conv = CONV_K // 2
    lane = lax.broadcasted_iota(jnp.int32, (SEQ_ROWS, POOL_W), 1)
    grp_w = POOL_W // POOL_GROUPS
    half = jnp.where(lane < grp_w, 1, jnp.where(lane < 2 * grp_w, 2, jnp.where(lane < 3 * grp_w, 4, 8)))
    for r0 in range(0, L, SEQ_ROWS):
        acc = jnp.zeros((SEQ_ROWS, CONV_W), F32) + cb
        for k in range(CONV_K):
            acc = acc + apad_ref[pl.ds(r0 + PAD_ROWS + k - half_conv, SEQ_ROWS), :] * cw_ref[pl.ds(k, 1), :]
        y = _layernorm(acc, clng, clnb)
        fb_ref[r0:r0 + SEQ_ROWS, :] = (y * _sigmoid(y)).astype(BF16)

        def tap(j):
            return ppad_ref[pl.ds(r0 + PAD_ROWS + j, SEQ_ROWS), :]
        x0 = tap(0)
        s2 = tap(-1) + x0
        s4 = s2 + tap(-2) + tap(1)
        s8 = s4 + tap(-4) + tap(-3) + tap(2) + tap(3)
        s16 = s8 + tap(-8) + tap(-7) + tap(-6) + tap(-5) + tap(4) + tap(5) + tap(6) + tap(7)
        ssum = jnp.where(lane < grp_w, s2, jnp.where(lane < 2 * grp_w, s4, jnp.where(lane < 3 * grp_w, s8, s16)))
        t = r0 + lax.broadcasted_iota(jnp.int32, (SEQ_ROWS, POOL_W), 0)
        cnt = jnp.minimum(t + half, L) - jnp.maximum(t - half, 0)
        pooled_ref[r0:r0 + SEQ_ROWS, :] = ssum / cnt.astype(F32) - x0
    fd_ref[...] = (_dot(pooled_ref[...].astype(BF16), pw_ref[...]) * ps_ref[...]).astype(BF16)


def _seqmix_call(a2d, pin2d, seq_len, lw):
    rows = a2d.shape[0]
    L = seq_len
    seq_spec = pl.BlockSpec((L, CONV_W), lambda b: (b, 0))
    return pl.pallas_call(
        functools.partial(_seqmix_kernel, L),
        grid=(rows // L,),
        in_specs=[seq_spec, seq_spec,
                  _const_spec((CONV_K, CONV_W)), _const_spec((1, CONV_W)),
                  _const_spec((1, CONV_W)), _const_spec((1, CONV_W)),
                  _const_spec((POOL_W, POOL_W)), _const_spec((1, POOL_W))],
        out_specs=[seq_spec, seq_spec],
        out_shape=[jax.ShapeDtypeStruct((rows, CONV_W), BF16), jax.ShapeDtypeStruct((rows, POOL_W), BF16)],
        scratch_shapes=[pltpu.VMEM((L + 2 * PAD_ROWS, CONV_W), F32),
                        pltpu.VMEM((L + 2 * PAD_ROWS, POOL_W), F32),
                        pltpu.VMEM((L, POOL_W), F32)],
        compiler_params=_params(1),
        name="seqmix",
    )(a2d, pin2d, lw["conv_dw"], lw["conv_dw_b"], lw["conv_ln_g"], lw["conv_ln_b"], lw["pool_bd"], lw["pool_scale"])


def _ctxkv_kernel(ckv_ref, kpe_ref, wkv_ref, e_ref, k_ref, v_ref):
    kvb = _dot(ckv_ref[...].astype(BF16), wkv_ref[...])
    k_ref[...] = (kvb[:, 0:QK_CAT] + _dot(kpe_ref[...].astype(BF16), e_ref[...])).astype(BF16)
    v_ref[...] = kvb[:, QK_CAT:QK_CAT + MLA_W].astype(BF16)


def _ctxkv_call(ckv2d, kpe2d, lw):
    rows = ckv2d.shape[0]
    return pl.pallas_call(
        _ctxkv_kernel,
        grid=(1,),
        in_specs=[pl.BlockSpec((rows, KV_LORA), lambda i: (0, 0)),
                  pl.BlockSpec((rows, LANES), lambda i: (0, 0)),
                  _const_spec(lw["wkv"].shape), _const_spec((LANES, QK_CAT))],
        out_specs=[pl.BlockSpec((rows, QK_CAT), lambda i: (0, 0)), pl.BlockSpec((rows, MLA_W), lambda i: (0, 0))],
        out_shape=[jax.ShapeDtypeStruct((rows, QK_CAT), BF16), jax.ShapeDtypeStruct((rows, MLA_W), BF16)],
        compiler_params=_params(1),
        name="ctxkv",
    )(ckv2d, kpe2d, lw["wkv"], lw["e_kpe"])


def _attn_kernel(nseg, *refs):
    q_refs = refs[0:nseg]
    k_refs = refs[nseg:2 * nseg]
    v_refs = refs[2 * nseg:3 * nseg]
    o_ref = refs[3 * nseg]
    for pair in range(MLA_HEADS // 2):
        plo = pair * LANES
        v_lo, v_hi = [], []
        for s in range(nseg):
            vp = v_refs[s][:, plo:plo + LANES]
            lane = lax.broadcasted_iota(jnp.int32, vp.shape, 1)
            v_lo.append(jnp.where(lane < V_HEAD, vp, jnp.zeros_like(vp)))
            v_hi.append(jnp.where(lane >= V_HEAD, vp, jnp.zeros_like(vp)))
        acc = None
        for hh, v_sel in ((0, v_lo), (1, v_hi)):
            lo = (2 * pair + hh) * HEAD_SLOT
            scores = [_dot_nt(q_refs[s][:, lo:lo + HEAD_SLOT], k_refs[s][:, lo:lo + HEAD_SLOT])
                      for s in range(nseg)]
            m = functools.reduce(jnp.maximum, [jnp.max(sc, axis=-1, keepdims=True) for sc in scores])
            probs = [jnp.exp(sc - m) for sc in scores]
            denom = functools.reduce(jnp.add, [jnp.sum(p, axis=-1, keepdims=True) for p in probs])
            o = functools.reduce(jnp.add, [_dot(p.astype(BF16), vs) for p, vs in zip(probs, v_sel)])
            o = o * (1.0 / denom)
            acc = o if acc is None else acc + o
        o_ref[:, plo:plo + LANES] = acc.astype(BF16)


def _attn_call(qs, ks, vs, seq_len):
    nseg = len(qs)
    rows = qs[0].shape[0]
    nb = rows // seq_len
    tq = ATTN_Q_TILE
    nq = seq_len // tq
    q_spec = pl.BlockSpec((tq, QK_CAT), lambda b, j: (b * nq + j, 0))
    k_specs = [pl.BlockSpec((k.shape[0] // nb, QK_CAT), lambda b, j: (b, 0)) for k in ks]
    v_specs = [pl.BlockSpec((v.shape[0] // nb, MLA_W), lambda b, j: (b, 0)) for v in vs]
    return pl.pallas_call(
        functools.partial(_attn_kernel, nseg),
        grid=(nb, nq),
        in_specs=[q_spec] * nseg + k_specs + v_specs,
        out_specs=pl.BlockSpec((tq, MLA_W), lambda b, j: (b * nq + j, 0)),
        out_shape=jax.ShapeDtypeStruct((rows, MLA_W), BF16),
        compiler_params=_params(2),
        name="attn%d" % nseg,
    )(*qs, *ks, *vs)


def _merge_kernel(x_ref, mod_ref, h_ref, fa_ref, fb_ref, fc_ref, fd_ref, wg_ref, bg_ref,
                  wa_ref, wb_ref, wc_ref, wd_ref, wout_ref, postg_ref, preg_ref, x1_ref, h2_ref):
    hb = h_ref[...]
    acc = None
    for k, (f_ref, w_ref) in enumerate(((fa_ref, wa_ref), (fb_ref, wb_ref), (fc_ref, wc_ref), (fd_ref, wd_ref))):
        gate = _sigmoid(_dot(hb, wg_ref[k]) + bg_ref[k])
        t = gate * _dot(f_ref[...], w_ref[...])
        acc = t if acc is None else acc + t
    mix = _dot(acc.astype(BF16), wout_ref[...])
    mod = mod_ref[...]
    g1 = mod[:, 2 * D_MODEL:3 * D_MODEL]
    sh2 = mod[:, 3 * D_MODEL:4 * D_MODEL]
    sc2 = mod[:, 4 * D_MODEL:5 * D_MODEL]
    x1 = x_ref[...] + g1 * _rms(mix, postg_ref[...])
    x1_ref[...] = x1
    h2_ref[...] = (_rms(x1, preg_ref[...]) * (1.0 + sc2) + sh2).astype(BF16)


def _merge_call(x2d, mod3, mod_row_of_tile, h, fa, fb, fc, fd, lw):
    rows = x2d.shape[0]
    tm = ROW_TILE
    row_spec = lambda w: pl.BlockSpec((tm, w), lambda i: (i, 0))
    return pl.pallas_call(
        _merge_kernel,
        grid=(rows // tm,),
        in_specs=[row_spec(D_MODEL),
                  pl.BlockSpec((None, 1, 6 * D_MODEL), lambda i: (mod_row_of_tile(i), 0, 0)),
                  row_spec(D_MODEL), row_spec(MLA_W), row_spec(CONV_W), row_spec(GMLP_W), row_spec(POOL_W),
                  _const_spec((N_BRANCH, D_MODEL, D_MODEL)), _const_spec((N_BRANCH, 1, D_MODEL)),
                  _const_spec((MLA_W, D_MODEL)), _const_spec((CONV_W, D_MODEL)),
                  _const_spec((GMLP_W, D_MODEL)), _const_spec((POOL_W, D_MODEL)),
                  _const_spec((D_MODEL, D_MODEL)), _const_spec((1, D_MODEL)), _const_spec((1, D_MODEL))],
        out_specs=[row_spec(D_MODEL), row_spec(D_MODEL)],
        out_shape=[jax.ShapeDtypeStruct((rows, D_MODEL), F32), jax.ShapeDtypeStruct((rows, D_MODEL), BF16)],
        compiler_params=_params(1),
        name="merge",
    )(x2d, mod3, h, fa, fb, fc, fd, lw["w_gate"], lw["b_gate"], lw["w_o_mla"], lw["w_conv_out"],
      lw["w_gmlp_out"], lw["w_pool_out"], lw["w_out"], lw["post_mix_g"], lw["pre_ffn_g"])


def _ffn_kernel(seq_len, h2_ref, x1_ref, mod_ref, wup_ref, dw_ref, dwb_ref, wdn_ref, postg_ref, o_ref):
    hb = h2_ref[...]
    tm = hb.shape[0]
    assert seq_len & (seq_len - 1) == 0
    pos = lax.broadcasted_iota(jnp.int32, (tm, 2 * FFN_CHUNK), 0) & (seq_len - 1)
    first = pos == 0
    last = pos == seq_len - 1
    o_ref[...] = jnp.zeros((tm, D_MODEL), F32)

    def body(c, carry):
        up = _dot(hb, wup_ref[c])
        dw = dw_ref[c]
        prev = jnp.where(first, 0.0, pltpu.roll(up, 1, 0))
        nxt = jnp.where(last, 0.0, pltpu.roll(up, tm - 1, 0))
        y = prev * dw[0:1, :] + up * dw[1:2, :] + nxt * dw[2:3, :] + dwb_ref[c]
        g = y[:, 0:FFN_CHUNK]
        act = (g * _sigmoid(g) * y[:, FFN_CHUNK:2 * FFN_CHUNK]).astype(BF16)
        o_ref[...] += _dot(act, wdn_ref[c])
        return carry

    lax.fori_loop(0, FFN_NCHUNK, body, 0)
    mod = mod_ref[...]
    g2 = mod[:, 5 * D_MODEL:6 * D_MODEL]
    o_ref[...] = x1_ref[...] + g2 * _rms(o_ref[...], postg_ref[...])


def _ffn_call(h2, x1, mod3, seq_len, mod_row_of_tile, lw):
    rows = h2.shape[0]
    tm = FFN_TILE
    row_spec = pl.BlockSpec((tm, D_MODEL), lambda i: (i, 0))
    return pl.pallas_call(
        functools.partial(_ffn_kernel, seq_len),
        grid=(rows // tm,),
        in_specs=[row_spec, row_spec,
                  pl.BlockSpec((None, 1, 6 * D_MODEL), lambda i: (mod_row_of_tile(i), 0, 0)),
                  _const_spec((FFN_NCHUNK, D_MODEL, 2 * FFN_CHUNK)),
                  _const_spec((FFN_NCHUNK, FFN_K, 2 * FFN_CHUNK)),
                  _const_spec((FFN_NCHUNK, 1, 2 * FFN_CHUNK)),
                  _const_spec((FFN_NCHUNK, FFN_CHUNK, D_MODEL)),
                  _const_spec((1, D_MODEL))],
        out_specs=row_spec,
        out_shape=jax.ShapeDtypeStruct((rows, D_MODEL), F32),
        compiler_params=_params(1),
        name="ffn",
    )(h2, x1, mod3, lw["ffn_up"], lw["ffn_dw"], lw["ffn_dw_b"], lw["ffn_down"], lw["post_ffn_g"])


def _rot_half_perm():
    src = np.zeros((QK_ROPE,), np.int32)
    sign = np.zeros((QK_ROPE,), np.float32)
    quarter = QK_ROPE // 4
    for d in range(QK_ROPE):
        e = d % (2 * quarter)
        if e < quarter:
            src[d], sign[d] = d + quarter, -1.0
        else:
            src[d], sign[d] = d - quarter, 1.0
    return src, sign


def _rope_tables(seq_len):
    quarter = QK_ROPE // 4
    inv = ROPE_BASE ** (-jnp.arange(quarter, dtype=F32) / quarter)
    t = jnp.arange(seq_len)
    row = (t // GRID_W).astype(F32)[:, None] * inv[None, :]
    col = (t % GRID_W).astype(F32)[:, None] * inv[None, :]
    ang = jnp.concatenate([row, row, col, col], axis=1)
    cos, sin = jnp.cos(ang), jnp.sin(ang)
    ones = jnp.ones((seq_len, QK_NOPE), F32)
    zeros_n = jnp.zeros((seq_len, QK_NOPE), F32)
    zeros_t = jnp.zeros((seq_len, HEAD_SLOT - QK_NOPE - QK_ROPE), F32)
    ccat = jnp.concatenate([ones, cos, zeros_t], axis=1)
    scat = jnp.concatenate([zeros_n, sin, zeros_t], axis=1)
    zeros_k = jnp.zeros((seq_len, LANES - QK_ROPE), F32)
    ckpe = jnp.concatenate([cos, zeros_k], axis=1)
    skpe = jnp.concatenate([sin, zeros_k], axis=1)
    return ccat, scat, ckpe, skpe


def _prep_layer(l, W):
    src, sign = _rot_half_perm()
    w_in = W["w_in"][l]
    kpe_w = w_in[:, OFF_KPE:OFF_CONV]
    kpe_wp = kpe_w[:, src] * sign[None, :]
    zpad = jnp.zeros((D_MODEL, LANES - QK_ROPE), F32)
    w_small = jnp.concatenate([w_in[:, 0:OFF_KPE], w_in[:, OFF_CONV:OFF_GATE], kpe_w, zpad, kpe_wp, zpad],
                              axis=1).astype(BF16)
    w_gate = w_in[:, OFF_GATE:].reshape(D_MODEL, N_BRANCH, D_MODEL).transpose(1, 0, 2).astype(BF16)

    wq = W["w_q_b"][l].reshape(Q_LORA, MLA_HEADS, QK_NOPE + QK_ROPE)
    q_nope, q_pe = wq[:, :, :QK_NOPE], wq[:, :, QK_NOPE:]
    tail = jnp.zeros((Q_LORA, MLA_HEADS, HEAD_SLOT - QK_NOPE - QK_ROPE), F32)
    wq_cat = jnp.concatenate([q_nope, q_pe, tail], axis=2).reshape(Q_LORA, QK_CAT)
    q_pe_p = q_pe[:, :, src] * sign[None, None, :]
    wq_rot = jnp.concatenate([jnp.zeros_like(q_nope), q_pe_p, tail], axis=2).reshape(Q_LORA, QK_CAT)
    wq_rope = jnp.concatenate([wq_cat, wq_rot], axis=1)

    wkv = W["w_kv_b"][l].reshape(KV_LORA, MLA_HEADS, QK_NOPE + V_HEAD)
    k_nope, v_w = wkv[:, :, :QK_NOPE], wkv[:, :, QK_NOPE:]
    k_cat = jnp.concatenate([k_nope, jnp.zeros((KV_LORA, MLA_HEADS, HEAD_SLOT - QK_NOPE), F32)], axis=2)
    wkv_cat = jnp.concatenate([k_cat.reshape(KV_LORA, QK_CAT), v_w.reshape(KV_LORA, MLA_W)], axis=1)

    e = np.zeros((LANES, QK_CAT), np.float32)
    for hd in range(MLA_HEADS):
        for d in range(QK_ROPE):
            e[d, hd * HEAD_SLOT + QK_NOPE + d] = 1.0

    nc, tf = FFN_NCHUNK, FFN_CHUNK

    def chunk_cols(a):
        g = a[..., :D_FF].reshape(a.shape[:-1] + (nc, tf))
        v = a[..., D_FF:].reshape(a.shape[:-1] + (nc, tf))
        return jnp.moveaxis(jnp.concatenate([g, v], axis=-1), -2, 0)

    row = lambda a: a.reshape(1, -1)
    return {
        "pre_mix_g": row(W["pre_mix_g"][l]), "post_mix_g": row(W["post_mix_g"][l]),
        "pre_ffn_g": row(W["pre_ffn_g"][l]), "post_ffn_g": row(W["post_ffn_g"][l]),
        "w_small": w_small, "w_gate": w_gate, "b_gate": W["b_gate"][l].reshape(N_BRANCH, 1, D_MODEL),
        "q_norm_g": row(W["q_norm_g"][l]), "wq": wq_cat.astype(BF16), "wq_rope": wq_rope.astype(BF16),
        "kv_norm_g": row(W["kv_norm_g"][l]), "wkv": wkv_cat.astype(BF16), "e_kpe": jnp.asarray(e, BF16),
        "w_o_mla": W["w_o_mla"][l].astype(BF16),
        "conv_dw": W["conv_dw"][l], "conv_dw_b": row(W["conv_dw_b"][l]),
        "conv_ln_g": row(W["conv_ln_g"][l]), "conv_ln_b": row(W["conv_ln_b"][l]),
        "w_conv_out": W["w_conv_out"][l].astype(BF16),
        "gmlp_ln_g": row(W["gmlp_ln_g"][l]), "gmlp_ln_b": row(W["gmlp_ln_b"][l]),
        "gmlp_wcat": W["gmlp_ws"][l].transpose(1, 0, 2).reshape(GMLP_CHUNK, GMLP_GROUPS * GMLP_CHUNK).astype(BF16),
        "gmlp_bmat": jnp.repeat(W["gmlp_bs"][l].T, GMLP_W // GMLP_GROUPS, axis=1),
        "w_gmlp_out": W["w_gmlp_out"][l].astype(BF16),
        "pool_bd": jax.scipy.linalg.block_diag(*[W["pool_w"][l, g] for g in range(POOL_GROUPS)]).astype(BF16),
        "pool_scale": row(W["pool_scale"][l]),
        "w_pool_out": W["w_pool_out"][l].astype(BF16),
        "w_out": W["w_out"][l].astype(BF16),
        "ffn_up": chunk_cols(W["ffn_up"][l]).astype(BF16),
        "ffn_dw": chunk_cols(W["ffn_dw"][l]),
        "ffn_dw_b": chunk_cols(W["ffn_dw_b"][l].reshape(1, -1)),
        "ffn_down": W["ffn_down"][l].reshape(nc, tf, D_MODEL).astype(BF16),
    }


def _trunk_layer(x2d, seq_len, mod3, mod_row_of_tile_small, mod_row_of_tile_ffn, lw, rope_tabs, ctx):
    p = _rowproj_call(x2d, mod3, seq_len, mod_row_of_tile_small, lw, rope_tabs)
    fb, fd = _seqmix_call(p["a"], p["pin"], seq_len, lw)
    if ctx is None:
        fa = _attn_call([p["q"]], [p["k"]], [p["v"]], seq_len)
    else:
        k_ctx, v_ctx = ctx
        fa = _attn_call([p["q"], p["qp"]], [p["k"], k_ctx], [p["v"], v_ctx], seq_len)
    x1, h2 = _merge_call(x2d, mod3, mod_row_of_tile_small, p["h"], fa, fb, p["fc"], fd, lw)
    x2 = _ffn_call(h2, x1, mod3, seq_len, mod_row_of_tile_ffn, lw)
    return x2, p["ckv"], p["kpe"]


def kernel(x_prompt, x_sample, cache_ckv, cache_kpe, c, c_ctx, ada_w, ada_b, pre_mix_g, post_mix_g, pre_ffn_g, post_ffn_g, w_in, b_gate, q_norm_g, w_q_b, kv_norm_g, w_kv_b, w_o_mla, conv_dw, conv_dw_b, conv_ln_g, conv_ln_b, w_conv_out, gmlp_ln_g, gmlp_ln_b, gmlp_ws, gmlp_bs, w_gmlp_out, pool_w, pool_scale, w_pool_out, w_out, ffn_up, ffn_dw, ffn_dw_b, ffn_down):
    W = dict(pre_mix_g=pre_mix_g, post_mix_g=post_mix_g, pre_ffn_g=pre_ffn_g, post_ffn_g=post_ffn_g,
             w_in=w_in, b_gate=b_gate, q_norm_g=q_norm_g, w_q_b=w_q_b, kv_norm_g=kv_norm_g,
             w_kv_b=w_kv_b, w_o_mla=w_o_mla, conv_dw=conv_dw, conv_dw_b=conv_dw_b,
             conv_ln_g=conv_ln_g, conv_ln_b=conv_ln_b, w_conv_out=w_conv_out, gmlp_ln_g=gmlp_ln_g,
             gmlp_ln_b=gmlp_ln_b, gmlp_ws=gmlp_ws, gmlp_bs=gmlp_bs, w_gmlp_out=w_gmlp_out,
             pool_w=pool_w, pool_scale=pool_scale, w_pool_out=w_pool_out, w_out=w_out,
             ffn_up=ffn_up, ffn_dw=ffn_dw, ffn_dw_b=ffn_dw_b, ffn_down=ffn_down)
    n_prompt, prompt_len, _ = x_prompt.shape
    n_sample, sample_len, _ = x_sample.shape
    past_len = cache_ckv.shape[2]
    ctx_row = n_sample

    cvec = jnp.zeros((SUBLANES, D_MODEL), F32).at[0:n_sample].set(c).at[ctx_row].set(c_ctx)
    mod = _mod_call(cvec, ada_w, ada_b)

    rope_tabs = _rope_tables(sample_len)
    xp = x_prompt.reshape(n_prompt * prompt_len, D_MODEL)
    xs = x_sample.reshape(n_sample * sample_len, D_MODEL)
    ckv_list, kpe_list = [], []
    for l in range(DEPTH):
        mod3 = mod[l].reshape(SUBLANES, 1, 6 * D_MODEL)
        lw = _prep_layer(l, W)
        xp, ckv_l, kpe_l = _trunk_layer(xp, prompt_len, mod3, lambda i: ctx_row, lambda i: ctx_row,
                                        lw, None, None)
        ckv_list.append(ckv_l.reshape(n_prompt, prompt_len, KV_LORA))
        kpe_list.append(kpe_l.reshape(n_prompt, prompt_len, QK_ROPE))
        kpe_pad = jnp.pad(cache_kpe[:, l].reshape(n_sample * past_len, QK_ROPE),
                          ((0, 0), (0, LANES - QK_ROPE)))
        ctx = _ctxkv_call(cache_ckv[:, l].reshape(n_sample * past_len, KV_LORA), kpe_pad, lw)
        xs, _, _ = _trunk_layer(xs, sample_len, mod3,
                                lambda i: (i * ROW_TILE) // sample_len,
                                lambda i: (i * FFN_TILE) // sample_len,
                                lw, rope_tabs, ctx)
    new_ckv = jnp.stack(ckv_list, axis=1)
    new_kpe = jnp.stack(kpe_list, axis=1)
    return (xp.reshape(n_prompt, prompt_len, D_MODEL), xs.reshape(n_sample, sample_len, D_MODEL),
            new_ckv, new_kpe)
```

```python
import functools

import jax
import jax.numpy as jnp
import numpy as np
from jax import lax
from jax.experimental import pallas as pl
from jax.experimental.pallas import tpu as pltpu

D_MODEL = 1024
DEPTH = 2
GRID_W = 64
MLA_HEADS = 8
Q_LORA = 384
KV_LORA = 256
QK_NOPE = 64
QK_ROPE = 32
V_HEAD = 64
MLA_W = MLA_HEADS * V_HEAD
SOFTMAX_SCALE = (QK_NOPE + QK_ROPE) ** -0.5
ROPE_BASE = 10000.0
CONV_W = 256
CONV_K = 31
GMLP_W = 256
GMLP_GROUPS = 4
GMLP_CHUNK = 128
POOL_W = 256
POOL_GROUPS = 4
N_BRANCH = 4
D_FF = 2816
FFN_K = 3
NORM_EPS = 1e-6
OFF_KV = Q_LORA
OFF_KPE = OFF_KV + KV_LORA
OFF_CONV = OFF_KPE + QK_ROPE
OFF_GMLP = OFF_CONV + 2 * CONV_W
OFF_POOL = OFF_GMLP + 2 * GMLP_W
OFF_GATE = OFF_POOL + POOL_W

LANES = 128
SUBLANES = 8
VMEM_LIMIT_BYTES = 56 * 1024 * 1024

S_Q = 0
S_KV = S_Q + Q_LORA
S_CONV = S_KV + KV_LORA
S_GMLP = S_CONV + 2 * CONV_W
S_POOL = S_GMLP + 2 * GMLP_W
S_KPE = S_POOL + POOL_W
S_KPEP = S_KPE + LANES
S_TOTAL = S_KPEP + LANES

HEAD_SLOT = LANES
QK_CAT = MLA_HEADS * HEAD_SLOT

ROW_TILE = 512
FFN_TILE = 1024
FFN_CHUNK = 256
FFN_NCHUNK = D_FF // FFN_CHUNK
SEQ_ROWS = 64
PAD_ROWS = 16
ATTN_Q_TILE = 256

BF16 = jnp.bfloat16
F32 = jnp.float32
INV_SQRT2 = 0.7071067811865476


def _dot(a, b):
    return jnp.dot(a, b, preferred_element_type=F32)


def _dot_nt(a, b):
    return lax.dot_general(a, b, (((1,), (1,)), ((), ())), preferred_element_type=F32)


def _rms(x, g):
    return x * lax.rsqrt(jnp.mean(x * x, axis=-1, keepdims=True) + NORM_EPS) * g


def _layernorm(x, g, b):
    mu = jnp.mean(x, axis=-1, keepdims=True)
    xc = x - mu
    var = jnp.mean(xc * xc, axis=-1, keepdims=True)
    return xc * lax.rsqrt(var + NORM_EPS) * g + b


def _sigmoid(x):
    return 1.0 / (1.0 + jnp.exp(-x))


def _const_spec(shape):
    nd = len(shape)
    return pl.BlockSpec(shape, lambda *_: (0,) * nd, pipeline_mode=pl.Buffered(1))


def _params(n_axes):
    return pltpu.CompilerParams(dimension_semantics=("arbitrary",) * n_axes,
                                vmem_limit_bytes=VMEM_LIMIT_BYTES)


def _mod_kernel(cv_ref, w_ref, b_ref, o_ref):
    cv = cv_ref[...]
    s = (cv * _sigmoid(cv)).astype(BF16)
    o_ref[0] = _dot(s, w_ref[0].astype(BF16)) + b_ref[0]


def _mod_call(cvec, ada_w, ada_b):
    tn = 1536
    n = 6 * D_MODEL
    return pl.pallas_call(
        _mod_kernel,
        grid=(DEPTH, n // tn),
        in_specs=[pl.BlockSpec((SUBLANES, D_MODEL), lambda l, j: (0, 0)),
                  pl.BlockSpec((1, D_MODEL, tn), lambda l, j: (l, 0, j)),
                  pl.BlockSpec((1, 1, tn), lambda l, j: (l, 0, j))],
        out_specs=pl.BlockSpec((1, SUBLANES, tn), lambda l, j: (l, 0, j)),
        out_shape=jax.ShapeDtypeStruct((DEPTH, SUBLANES, n), F32),
        compiler_params=_params(2),
        name="mod",
    )(cvec, ada_w, ada_b.reshape(DEPTH, 1, n))


def _rowproj_kernel(rope, x_ref, mod_ref, preg_ref, ws_ref, qng_ref, wq_ref, kvng_ref, wkv_ref, e_ref,
                    glng_ref, glnb_ref, gws_ref, gbm_ref, *rest):
    if rope:
        ccat_ref, scat_ref, ckpe_ref, skpe_ref = rest[:4]
        rest = rest[4:]
        h_ref, q_ref, qp_ref, k_ref, v_ref, ckv_ref, kpe_ref, a_ref, pin_ref, fc_ref = rest
    else:
        h_ref, q_ref, k_ref, v_ref, ckv_ref, kpe_ref, a_ref, pin_ref, fc_ref = rest
    x = x_ref[...]
    tm = x.shape[0]
    mod = mod_ref[...]
    sh1 = mod[:, 0:D_MODEL]
    sc1 = mod[:, D_MODEL:2 * D_MODEL]
    h = _rms(x, preg_ref[...]) * (1.0 + sc1) + sh1
    hb = h.astype(BF16)
    h_ref[...] = hb

    qn = _rms(_dot(hb, ws_ref[:, S_Q:S_KV]), qng_ref[...]).astype(BF16)
    qq = _dot(qn, wq_ref[...]) * SOFTMAX_SCALE
    if rope:
        ccat = ccat_ref[...]
        scat = scat_ref[...]
        for hd in range(MLA_HEADS):
            lo = hd * HEAD_SLOT
            qpl = qq[:, lo:lo + HEAD_SLOT]
            qpr = qq[:, QK_CAT + lo:QK_CAT + lo + HEAD_SLOT]
            q_ref[:, lo:lo + HEAD_SLOT] = (qpl * ccat + qpr * scat).astype(BF16)
            qp_ref[:, lo:lo + HEAD_SLOT] = qpl.astype(BF16)
    else:
        q_ref[...] = qq.astype(BF16)
    ckv = _rms(_dot(hb, ws_ref[:, S_KV:S_CONV]), kvng_ref[...])
    ckv_ref[...] = ckv
    kvb = _dot(ckv.astype(BF16), wkv_ref[...])
    kpa = _dot(hb, ws_ref[:, S_KPE:S_KPEP])
    kpe_ref[...] = kpa[:, 0:QK_ROPE]
    if rope:
        kpr = kpa * ckpe_ref[...] + _dot(hb, ws_ref[:, S_KPEP:S_TOTAL]) * skpe_ref[...]
    else:
        kpr = kpa
    k_ref[...] = (kvb[:, 0:QK_CAT] + _dot(kpr.astype(BF16), e_ref[...])).astype(BF16)
    v_ref[...] = kvb[:, QK_CAT:QK_CAT + MLA_W].astype(BF16)

    ci = _dot(hb, ws_ref[:, S_CONV:S_GMLP])
    a_ref[...] = ci[:, 0:CONV_W] * _sigmoid(ci[:, CONV_W:2 * CONV_W])
    pin_ref[...] = _dot(hb, ws_ref[:, S_POOL:S_KPE])

    gm = _dot(hb, ws_ref[:, S_GMLP:S_POOL])
    uv = 0.5 * gm * (1.0 + lax.erf(gm * INV_SQRT2))
    u = uv[:, 0:GMLP_W]
    v = _layernorm(uv[:, GMLP_W:2 * GMLP_W], glng_ref[...], glnb_ref[...])
    grp = lax.broadcasted_iota(jnp.int32, (GMLP_CHUNK, GMLP_W), 1) // (GMLP_W // GMLP_GROUPS)
    gws = gws_ref[...]
    gbm = gbm_ref[...]
    for n in range(tm // GMLP_CHUNK):
        r0 = n * GMLP_CHUNK
        vc = v[r0:r0 + GMLP_CHUNK]
        vs = jnp.concatenate([jnp.where(grp == g, vc, 0.0) for g in range(GMLP_GROUPS)], axis=0)
        sv = _dot(gws, vs.astype(BF16)) + gbm
        fc_ref[r0:r0 + GMLP_CHUNK, :] = (u[r0:r0 + GMLP_CHUNK] * sv).astype(BF16)


def _rowproj_call(x2d, mod3, seq_len, mod_row_of_tile, lw, rope_tabs):
    rows = x2d.shape[0]
    tm = ROW_TILE
    rope = rope_tabs is not None
    wq = lw["wq_rope"] if rope else lw["wq"]
    row_spec = lambda w: pl.BlockSpec((tm, w), lambda i: (i, 0))
    in_specs = [row_spec(D_MODEL),
                pl.BlockSpec((None, 1, 6 * D_MODEL), lambda i: (mod_row_of_tile(i), 0, 0)),
                _const_spec((1, D_MODEL)),
                _const_spec((D_MODEL, S_TOTAL)),
                _const_spec((1, Q_LORA)),
                _const_spec(wq.shape),
                _const_spec((1, KV_LORA)),
                _const_spec(lw["wkv"].shape),
                _const_spec((LANES, QK_CAT)),
                _const_spec((1, GMLP_W)), _const_spec((1, GMLP_W)),
                _const_spec((GMLP_CHUNK, GMLP_GROUPS * GMLP_CHUNK)),
                _const_spec((GMLP_CHUNK, GMLP_W))]
    args = [x2d, mod3, lw["pre_mix_g"], lw["w_small"], lw["q_norm_g"], wq, lw["kv_norm_g"], lw["wkv"],
            lw["e_kpe"], lw["gmlp_ln_g"], lw["gmlp_ln_b"], lw["gmlp_wcat"], lw["gmlp_bmat"]]
    outs = [("h", D_MODEL, BF16), ("q", QK_CAT, BF16)]
    if rope:
        tiles_per_seq = seq_len // tm
        tab_spec = pl.BlockSpec((tm, LANES), lambda i: (i % tiles_per_seq, 0))
        in_specs += [tab_spec] * 4
        args += list(rope_tabs)
        outs.append(("qp", QK_CAT, BF16))
    outs += [("k", QK_CAT, BF16), ("v", MLA_W, BF16), ("ckv", KV_LORA, F32), ("kpe", QK_ROPE, F32),
             ("a", CONV_W, F32), ("pin", POOL_W, F32), ("fc", GMLP_W, BF16)]
    res = pl.pallas_call(
        functools.partial(_rowproj_kernel, rope),
        grid=(rows // tm,),
        in_specs=in_specs,
        out_specs=[row_spec(w) for _, w, _ in outs],
        out_shape=[jax.ShapeDtypeStruct((rows, w), dt) for _, w, dt in outs],
        compiler_params=_params(1),
        name="rowproj_rope" if rope else "rowproj",
    )(*args)
    return {name: r for (name, _, _), r in zip(outs, res)}


def _seqmix_kernel(seq_len, a_ref, pin_ref, cw_ref, cb_ref, clng_ref, clnb_ref, pw_ref, ps_ref,
                   fb_ref, fd_ref, apad_ref, ppad_ref, pooled_ref):
    L = seq_len
    zeros = jnp.zeros((PAD_ROWS, CONV_W), F32)
    for pad_ref, src_ref in ((apad_ref, a_ref), (ppad_ref, pin_ref)):
        pad_ref[0:PAD_ROWS, :] = zeros
        pad_ref[L + PAD_ROWS:L + 2 * PAD_ROWS, :] = zeros
        pad_ref[PAD_ROWS:L + PAD_ROWS, :] = src_ref[...]
    cb = cb_ref[...]
    clng = clng_ref[...]
    clnb = clnb_ref[...]
    half_conv = CONV_K // 2
    lane = lax.broadcasted_iota(jnp.int32, (SEQ_ROWS, POOL_W), 1)
    grp_w = POOL_W // POOL_GROUPS
    half = jnp.where(lane < grp_w, 1, jnp.where(lane < 2 * grp_w, 2, jnp.where(lane < 3 * grp_w, 4, 8)))
    win = SEQ_ROWS + 2 * PAD_ROWS

    def shifted_windows(pad_ref, r0):
        w = pad_ref[r0:r0 + win, :]
        return [w] + [pltpu.roll(w, win - b, 0) for b in range(1, SUBLANES)]

    def tap_of(xs, off):
        a = off // SUBLANES * SUBLANES
        return xs[off % SUBLANES][a:a + SEQ_ROWS]

    for r0 in range(0, L, SEQ_ROWS):
        xs = shifted_windows(apad_ref, r0)
        acc = jnp.zeros((SEQ_ROWS, CONV_W), F32) + cb
        for k in range(CONV_K):
            acc = acc + tap_of(xs, PAD_ROWS + k - half_conv) * cw_ref[pl.ds(k, 1), :]
        y = _layernorm(acc, clng, clnb)
        fb_ref[r0:r0 + SEQ_ROWS, :] = (y * _sigmoid(y)).astype(BF16)

        ps = shifted_windows(ppad_ref, r0)

        def tap(j):
            return tap_of(ps, PAD_ROWS + j)
        x0 = tap(0)
        s2 = tap(-1) + x0
        s4 = s2 + tap(-2) + tap(1)
        s8 = s4 + tap(-4) + tap(-3) + tap(2) + tap(3)
        s16 = s8 + tap(-8) + tap(-7) + tap(-6) + tap(-5) + tap(4) + tap(5) + tap(6) + tap(7)
        ssum = jnp.where(lane < grp_w, s2, jnp.where(lane < 2 * grp_w, s4, jnp.where(lane < 3 * grp_w, s8, s16)))
        t = r0 + lax.broadcasted_iota(jnp.int32, (SEQ_ROWS, POOL_W), 0)
        cnt = jnp.minimum(t + half, L) - jnp.maximum(t - half, 0)
        pooled_ref[r0:r0 + SEQ_ROWS, :] = ssum / cnt.astype(F32) - x0
    fd_ref[...] = (_dot(pooled_ref[...].astype(BF16), pw_ref[...]) * ps_ref[...]).astype(BF16)


def _seqmix_call(a2d, pin2d, seq_len, lw):
    rows = a2d.shape[0]
    L = seq_len
    seq_spec = pl.BlockSpec((L, CONV_W), lambda b: (b, 0))
    return pl.pallas_call(
        functools.partial(_seqmix_kernel, L),
        grid=(rows // L,),
        in_specs=[seq_spec, seq_spec,
                  _const_spec((CONV_K, CONV_W)), _const_spec((1, CONV_W)),
                  _const_spec((1, CONV_W)), _const_spec((1, CONV_W)),
                  _const_spec((POOL_W, POOL_W)), _const_spec((1, POOL_W))],
        out_specs=[seq_spec, seq_spec],
        out_shape=[jax.ShapeDtypeStruct((rows, CONV_W), BF16), jax.ShapeDtypeStruct((rows, POOL_W), BF16)],
        scratch_shapes=[pltpu.VMEM((L + 2 * PAD_ROWS, CONV_W), F32),
                        pltpu.VMEM((L + 2 * PAD_ROWS, POOL_W), F32),
                        pltpu.VMEM((L, POOL_W), F32)],
        compiler_params=_params(1),
        name="seqmix",
    )(a2d, pin2d, lw["conv_dw"], lw["conv_dw_b"], lw["conv_ln_g"], lw["conv_ln_b"], lw["pool_bd"], lw["pool_scale"])


def _ctxkv_kernel(ckv_ref, kpe_ref, wkv_ref, e_ref, k_ref, v_ref):
    kvb = _dot(ckv_ref[...].astype(BF16), wkv_ref[...])
    k_ref[...] = (kvb[:, 0:QK_CAT] + _dot(kpe_ref[...].astype(BF16), e_ref[...])).astype(BF16)
    v_ref[...] = kvb[:, QK_CAT:QK_CAT + MLA_W].astype(BF16)


def _ctxkv_call(ckv2d, kpe2d, lw):
    rows = ckv2d.shape[0]
    return pl.pallas_call(
        _ctxkv_kernel,
        grid=(1,),
        in_specs=[pl.BlockSpec((rows, KV_LORA), lambda i: (0, 0)),
                  pl.BlockSpec((rows, LANES), lambda i: (0, 0)),
                  _const_spec(lw["wkv"].shape), _const_spec((LANES, QK_CAT))],
        out_specs=[pl.BlockSpec((rows, QK_CAT), lambda i: (0, 0)), pl.BlockSpec((rows, MLA_W), lambda i: (0, 0))],
        out_shape=[jax.ShapeDtypeStruct((rows, QK_CAT), BF16), jax.ShapeDtypeStruct((rows, MLA_W), BF16)],
        compiler_params=_params(1),
        name="ctxkv",
    )(ckv2d, kpe2d, lw["wkv"], lw["e_kpe"])


def _attn_kernel(nseg, *refs):
    q_refs = refs[0:nseg]
    k_refs = refs[nseg:2 * nseg]
    v_refs = refs[2 * nseg:3 * nseg]
    o_ref = refs[3 * nseg]
    for pair in range(MLA_HEADS // 2):
        plo = pair * LANES
        v_lo, v_hi = [], []
        for s in range(nseg):
            vp = v_refs[s][:, plo:plo + LANES]
            lane = lax.broadcasted_iota(jnp.int32, vp.shape, 1)
            v_lo.append(jnp.where(lane < V_HEAD, vp, jnp.zeros_like(vp)))
            v_hi.append(jnp.where(lane >= V_HEAD, vp, jnp.zeros_like(vp)))
        acc = None
        for hh, v_sel in ((0, v_lo), (1, v_hi)):
            lo = (2 * pair + hh) * HEAD_SLOT
            scores = [_dot_nt(q_refs[s][:, lo:lo + HEAD_SLOT], k_refs[s][:, lo:lo + HEAD_SLOT])
                      for s in range(nseg)]
            m = functools.reduce(jnp.maximum, [jnp.max(sc, axis=-1, keepdims=True) for sc in scores])
            probs = [jnp.exp(sc - m) for sc in scores]
            denom = functools.reduce(jnp.add, [jnp.sum(p, axis=-1, keepdims=True) for p in probs])
            o = functools.reduce(jnp.add, [_dot(p.astype(BF16), vs) for p, vs in zip(probs, v_sel)])
            o = o * (1.0 / denom)
            acc = o if acc is None else acc + o
        o_ref[:, plo:plo + LANES] = acc.astype(BF16)


def _attn_call(qs, ks, vs, seq_len):
    nseg = len(qs)
    rows = qs[0].shape[0]
    nb = rows // seq_len
    tq = ATTN_Q_TILE
    nq = seq_len // tq
    q_spec = pl.BlockSpec((tq, QK_CAT), lambda b, j: (b * nq + j, 0))
    k_specs = [pl.BlockSpec((k.shape[0] // nb, QK_CAT), lambda b, j: (b, 0)) for k in ks]
    v_specs = [pl.BlockSpec((v.shape[0] // nb, MLA_W), lambda b, j: (b, 0)) for v in vs]
    return pl.pallas_call(
        functools.partial(_attn_kernel, nseg),
        grid=(nb, nq),
        in_specs=[q_spec] * nseg + k_specs + v_specs,
        out_specs=pl.BlockSpec((tq, MLA_W), lambda b, j: (b * nq + j, 0)),
        out_shape=jax.ShapeDtypeStruct((rows, MLA_W), BF16),
        compiler_params=_params(2),
        name="attn%d" % nseg,
    )(*qs, *ks, *vs)


def _merge_kernel(x_ref, mod_ref, h_ref, fa_ref, fb_ref, fc_ref, fd_ref, wg_ref, bg_ref,
                  wa_ref, wb_ref, wc_ref, wd_ref, wout_ref, postg_ref, preg_ref, x1_ref, h2_ref):
    hb = h_ref[...]
    acc = None
    for k, (f_ref, w_ref) in enumerate(((fa_ref, wa_ref), (fb_ref, wb_ref), (fc_ref, wc_ref), (fd_ref, wd_ref))):
        cols = slice(k * D_MODEL, (k + 1) * D_MODEL)
        gate = _sigmoid(_dot(hb, wg_ref[:, cols]) + bg_ref[:, cols])
        t = gate * _dot(f_ref[...], w_ref[...])
        acc = t if acc is None else acc + t
    mix = _dot(acc.astype(BF16), wout_ref[...])
    mod = mod_ref[...]
    g1 = mod[:, 2 * D_MODEL:3 * D_MODEL]
    sh2 = mod[:, 3 * D_MODEL:4 * D_MODEL]
    sc2 = mod[:, 4 * D_MODEL:5 * D_MODEL]
    x1 = x_ref[...] + g1 * _rms(mix, postg_ref[...])
    x1_ref[...] = x1
    h2_ref[...] = (_rms(x1, preg_ref[...]) * (1.0 + sc2) + sh2).astype(BF16)


def _merge_call(x2d, mod3, mod_row_of_tile, h, fa, fb, fc, fd, lw):
    rows = x2d.shape[0]
    tm = ROW_TILE
    row_spec = lambda w: pl.BlockSpec((tm, w), lambda i: (i, 0))
    return pl.pallas_call(
        _merge_kernel,
        grid=(rows // tm,),
        in_specs=[row_spec(D_MODEL),
                  pl.BlockSpec((None, 1, 6 * D_MODEL), lambda i: (mod_row_of_tile(i), 0, 0)),
                  row_spec(D_MODEL), row_spec(MLA_W), row_spec(CONV_W), row_spec(GMLP_W), row_spec(POOL_W),
                  _const_spec((D_MODEL, N_BRANCH * D_MODEL)), _const_spec((1, N_BRANCH * D_MODEL)),
                  _const_spec((MLA_W, D_MODEL)), _const_spec((CONV_W, D_MODEL)),
                  _const_spec((GMLP_W, D_MODEL)), _const_spec((POOL_W, D_MODEL)),
                  _const_spec((D_MODEL, D_MODEL)), _const_spec((1, D_MODEL)), _const_spec((1, D_MODEL))],
        out_specs=[row_spec(D_MODEL), row_spec(D_MODEL)],
        out_shape=[jax.ShapeDtypeStruct((rows, D_MODEL), F32), jax.ShapeDtypeStruct((rows, D_MODEL), BF16)],
        compiler_params=_params(1),
        name="merge",
    )(x2d, mod3, h, fa, fb, fc, fd, lw["w_gate"], lw["b_gate"], lw["w_o_mla"], lw["w_conv_out"],
      lw["w_gmlp_out"], lw["w_pool_out"], lw["w_out"], lw["post_mix_g"], lw["pre_ffn_g"])


def _ffn_kernel(seq_len, h2_ref, x1_ref, mod_ref, wup_ref, dw_ref, dwb_ref, wdn_ref, postg_ref, o_ref,
                act_ref, upa_ref, upb_ref):
    tm = h2_ref.shape[0]
    assert seq_len & (seq_len - 1) == 0 and FFN_NCHUNK % 2 == 1
    pos = lax.broadcasted_iota(jnp.int32, (tm, FFN_CHUNK), 0) & (seq_len - 1)
    first = pos == 0
    last = pos == seq_len - 1
    halves = (slice(0, FFN_CHUNK), slice(FFN_CHUNK, 2 * FFN_CHUNK))

    def chunk_cols(c):
        c0 = pl.multiple_of(c * FFN_CHUNK, FFN_CHUNK)
        return pl.ds(c0, FFN_CHUNK), pl.ds(pl.multiple_of(c0 + D_FF, FFN_CHUNK), FFN_CHUNK)

    def project(c, up_ref):
        for half, cols in zip(halves, chunk_cols(c)):
            up_ref[:, half] = _dot(h2_ref[...], wup_ref[:, cols])

    def consume(c, up_ref):
        ys = []
        for half, cols in zip(halves, chunk_cols(c)):
            up = up_ref[:, half]
            dw = dw_ref[:, cols]
            prev = jnp.where(first, 0.0, pltpu.roll(up, 1, 0))
            nxt = jnp.where(last, 0.0, pltpu.roll(up, tm - 1, 0))
            ys.append(prev * dw[0:1, :] + up * dw[1:2, :] + nxt * dw[2:3, :] + dwb_ref[:, cols])
        g, val = ys
        act_ref[:, chunk_cols(c)[0]] = (g * _sigmoid(g) * val).astype(BF16)

    project(0, upa_ref)

    def body(i, carry):
        c = 2 * i
        project(c + 1, upb_ref)
        consume(c, upa_ref)
        project(c + 2, upa_ref)
        consume(c + 1, upb_ref)
        return carry

    lax.fori_loop(0, FFN_NCHUNK // 2, body, 0)
    consume(FFN_NCHUNK - 1, upa_ref)
    f = _dot(act_ref[...], wdn_ref[...])
    g2 = mod_ref[...][:, 5 * D_MODEL:6 * D_MODEL]
    o_ref[...] = x1_ref[...] + g2 * _rms(f, postg_ref[...])


def _ffn_call(h2, x1, mod3, seq_len, mod_row_of_tile, lw):
    rows = h2.shape[0]
    tm = FFN_TILE
    row_spec = pl.BlockSpec((tm, D_MODEL), lambda i: (i, 0))
    return pl.pallas_call(
        functools.partial(_ffn_kernel, seq_len),
        grid=(rows // tm,),
        in_specs=[row_spec, row_spec,
                  pl.BlockSpec((None, 1, 6 * D_MODEL), lambda i: (mod_row_of_tile(i), 0, 0)),
                  _const_spec((D_MODEL, 2 * D_FF)),
                  _const_spec((FFN_K, 2 * D_FF)),
                  _const_spec((1, 2 * D_FF)),
                  _const_spec((D_FF, D_MODEL)),
                  _const_spec((1, D_MODEL))],
        out_specs=row_spec,
        out_shape=jax.ShapeDtypeStruct((rows, D_MODEL), F32),
        scratch_shapes=[pltpu.VMEM((tm, D_FF), BF16),
                        pltpu.VMEM((tm, 2 * FFN_CHUNK), F32), pltpu.VMEM((tm, 2 * FFN_CHUNK), F32)],
        compiler_params=_params(1),
        name="ffn",
    )(h2, x1, mod3, lw["ffn_up"], lw["ffn_dw"], lw["ffn_dw_b"], lw["ffn_down"], lw["post_ffn_g"])


def _rot_half_perm():
    src = np.zeros((QK_ROPE,), np.int32)
    sign = np.zeros((QK_ROPE,), np.float32)
    quarter = QK_ROPE // 4
    for d in range(QK_ROPE):
        e = d % (2 * quarter)
        if e < quarter:
            src[d], sign[d] = d + quarter, -1.0
        else:
            src[d], sign[d] = d - quarter, 1.0
    return src, sign


def _rope_tables(seq_len):
    quarter = QK_ROPE // 4
    inv = ROPE_BASE ** (-jnp.arange(quarter, dtype=F32) / quarter)
    t = jnp.arange(seq_len)
    row = (t // GRID_W).astype(F32)[:, None] * inv[None, :]
    col = (t % GRID_W).astype(F32)[:, None] * inv[None, :]
    ang = jnp.concatenate([row, row, col, col], axis=1)
    cos, sin = jnp.cos(ang), jnp.sin(ang)
    ones = jnp.ones((seq_len, QK_NOPE), F32)
    zeros_n = jnp.zeros((seq_len, QK_NOPE), F32)
    zeros_t = jnp.zeros((seq_len, HEAD_SLOT - QK_NOPE - QK_ROPE), F32)
    ccat = jnp.concatenate([ones, cos, zeros_t], axis=1)
    scat = jnp.concatenate([zeros_n, sin, zeros_t], axis=1)
    zeros_k = jnp.zeros((seq_len, LANES - QK_ROPE), F32)
    ckpe = jnp.concatenate([cos, zeros_k], axis=1)
    skpe = jnp.concatenate([sin, zeros_k], axis=1)
    return ccat, scat, ckpe, skpe


def _prep_layer(l, W):
    src, sign = _rot_half_perm()
    w_in = W["w_in"][l]
    kpe_w = w_in[:, OFF_KPE:OFF_CONV]
    kpe_wp = kpe_w[:, src] * sign[None, :]
    zpad = jnp.zeros((D_MODEL, LANES - QK_ROPE), F32)
    w_small = jnp.concatenate([w_in[:, 0:OFF_KPE], w_in[:, OFF_CONV:OFF_GATE], kpe_w, zpad, kpe_wp, zpad],
                              axis=1).astype(BF16)
    w_gate = w_in[:, OFF_GATE:].astype(BF16)

    wq = W["w_q_b"][l].reshape(Q_LORA, MLA_HEADS, QK_NOPE + QK_ROPE)
    q_nope, q_pe = wq[:, :, :QK_NOPE], wq[:, :, QK_NOPE:]
    tail = jnp.zeros((Q_LORA, MLA_HEADS, HEAD_SLOT - QK_NOPE - QK_ROPE), F32)
    wq_cat = jnp.concatenate([q_nope, q_pe, tail], axis=2).reshape(Q_LORA, QK_CAT)
    q_pe_p = q_pe[:, :, src] * sign[None, None, :]
    wq_rot = jnp.concatenate([jnp.zeros_like(q_nope), q_pe_p, tail], axis=2).reshape(Q_LORA, QK_CAT)
    wq_rope = jnp.concatenate([wq_cat, wq_rot], axis=1)

    wkv = W["w_kv_b"][l].reshape(KV_LORA, MLA_HEADS, QK_NOPE + V_HEAD)
    k_nope, v_w = wkv[:, :, :QK_NOPE], wkv[:, :, QK_NOPE:]
    k_cat = jnp.concatenate([k_nope, jnp.zeros((KV_LORA, MLA_HEADS, HEAD_SLOT - QK_NOPE), F32)], axis=2)
    wkv_cat = jnp.concatenate([k_cat.reshape(KV_LORA, QK_CAT), v_w.reshape(KV_LORA, MLA_W)], axis=1)

    e = np.zeros((LANES, QK_CAT), np.float32)
    for hd in range(MLA_HEADS):
        for d in range(QK_ROPE):
            e[d, hd * HEAD_SLOT + QK_NOPE + d] = 1.0

    row = lambda a: a.reshape(1, -1)
    return {
        "pre_mix_g": row(W["pre_mix_g"][l]), "post_mix_g": row(W["post_mix_g"][l]),
        "pre_ffn_g": row(W["pre_ffn_g"][l]), "post_ffn_g": row(W["post_ffn_g"][l]),
        "w_small": w_small, "w_gate": w_gate, "b_gate": row(W["b_gate"][l]),
        "q_norm_g": row(W["q_norm_g"][l]), "wq": wq_cat.astype(BF16), "wq_rope": wq_rope.astype(BF16),
        "kv_norm_g": row(W["kv_norm_g"][l]), "wkv": wkv_cat.astype(BF16), "e_kpe": jnp.asarray(e, BF16),
        "w_o_mla": W["w_o_mla"][l].astype(BF16),
        "conv_dw": W["conv_dw"][l], "conv_dw_b": row(W["conv_dw_b"][l]),
        "conv_ln_g": row(W["conv_ln_g"][l]), "conv_ln_b": row(W["conv_ln_b"][l]),
        "w_conv_out": W["w_conv_out"][l].astype(BF16),
        "gmlp_ln_g": row(W["gmlp_ln_g"][l]), "gmlp_ln_b": row(W["gmlp_ln_b"][l]),
        "gmlp_wcat": W["gmlp_ws"][l].transpose(1, 0, 2).reshape(GMLP_CHUNK, GMLP_GROUPS * GMLP_CHUNK).astype(BF16),
        "gmlp_bmat": jnp.repeat(W["gmlp_bs"][l].T, GMLP_W // GMLP_GROUPS, axis=1),
        "w_gmlp_out": W["w_gmlp_out"][l].astype(BF16),
        "pool_bd": jax.scipy.linalg.block_diag(*[W["pool_w"][l, g] for g in range(POOL_GROUPS)]).astype(BF16),
        "pool_scale": row(W["pool_scale"][l]),
        "w_pool_out": W["w_pool_out"][l].astype(BF16),
        "w_out": W["w_out"][l].astype(BF16),
        "ffn_up": W["ffn_up"][l].astype(BF16),
        "ffn_dw": W["ffn_dw"][l],
        "ffn_dw_b": row(W["ffn_dw_b"][l]),
        "ffn_down": W["ffn_down"][l].astype(BF16),
    }


def _trunk_layer(x2d, seq_len, mod3, mod_row_of_tile_small, mod_row_of_tile_ffn, lw, rope_tabs, ctx):
    p = _rowproj_call(x2d, mod3, seq_len, mod_row_of_tile_small, lw, rope_tabs)
    fb, fd = _seqmix_call(p["a"], p["pin"], seq_len, lw)
    if ctx is None:
        fa = _attn_call([p["q"]], [p["k"]], [p["v"]], seq_len)
    else:
        k_ctx, v_ctx = ctx
        fa = _attn_call([p["q"], p["qp"]], [p["k"], k_ctx], [p["v"], v_ctx], seq_len)
    x1, h2 = _merge_call(x2d, mod3, mod_row_of_tile_small, p["h"], fa, fb, p["fc"], fd, lw)
    x2 = _ffn_call(h2, x1, mod3, seq_len, mod_row_of_tile_ffn, lw)
    return x2, p["ckv"], p["kpe"]


def kernel(x_prompt, x_sample, cache_ckv, cache_kpe, c, c_ctx, ada_w, ada_b, pre_mix_g, post_mix_g, pre_ffn_g, post_ffn_g, w_in, b_gate, q_norm_g, w_q_b, kv_norm_g, w_kv_b, w_o_mla, conv_dw, conv_dw_b, conv_ln_g, conv_ln_b, w_conv_out, gmlp_ln_g, gmlp_ln_b, gmlp_ws, gmlp_bs, w_gmlp_out, pool_w, pool_scale, w_pool_out, w_out, ffn_up, ffn_dw, ffn_dw_b, ffn_down):
    W = dict(pre_mix_g=pre_mix_g, post_mix_g=post_mix_g, pre_ffn_g=pre_ffn_g, post_ffn_g=post_ffn_g,
             w_in=w_in, b_gate=b_gate, q_norm_g=q_norm_g, w_q_b=w_q_b, kv_norm_g=kv_norm_g,
             w_kv_b=w_kv_b, w_o_mla=w_o_mla, conv_dw=conv_dw, conv_dw_b=conv_dw_b,
             conv_ln_g=conv_ln_g, conv_ln_b=conv_ln_b, w_conv_out=w_conv_out, gmlp_ln_g=gmlp_ln_g,
             gmlp_ln_b=gmlp_ln_b, gmlp_ws=gmlp_ws, gmlp_bs=gmlp_bs, w_gmlp_out=w_gmlp_out,
             pool_w=pool_w, pool_scale=pool_scale, w_pool_out=w_pool_out, w_out=w_out,
             ffn_up=ffn_up, ffn_dw=ffn_dw, ffn_dw_b=ffn_dw_b, ffn_down=ffn_down)
    n_prompt, prompt_len, _ = x_prompt.shape
    n_sample, sample_len, _ = x_sample.shape
    past_len = cache_ckv.shape[2]
    ctx_row = n_sample

    cvec = jnp.zeros((SUBLANES, D_MODEL), F32).at[0:n_sample].set(c).at[ctx_row].set(c_ctx)
    mod = _mod_call(cvec, ada_w, ada_b)

    rope_tabs = _rope_tables(sample_len)
    xp = x_prompt.reshape(n_prompt * prompt_len, D_MODEL)
    xs = x_sample.reshape(n_sample * sample_len, D_MODEL)
    ckv_list, kpe_list = [], []
    for l in range(DEPTH):
        mod3 = mod[l].reshape(SUBLANES, 1, 6 * D_MODEL)
        lw = _prep_layer(l, W)
        xp, ckv_l, kpe_l = _trunk_layer(xp, prompt_len, mod3, lambda i: ctx_row, lambda i: ctx_row,
                                        lw, None, None)
        ckv_list.append(ckv_l.reshape(n_prompt, prompt_len, KV_LORA))
        kpe_list.append(kpe_l.reshape(n_prompt, prompt_len, QK_ROPE))
        kpe_pad = jnp.pad(cache_kpe[:, l].reshape(n_sample * past_len, QK_ROPE),
                          ((0, 0), (0, LANES - QK_ROPE)))
        ctx = _ctxkv_call(cache_ckv[:, l].reshape(n_sample * past_len, KV_LORA), kpe_pad, lw)
        xs, _, _ = _trunk_layer(xs, sample_len, mod3,
                                lambda i: (i * ROW_TILE) // sample_len,
                                lambda i: (i * FFN_TILE) // sample_len,
                                lw, rope_tabs, ctx)
    new_ckv = jnp.stack(ckv_list, axis=1)
    new_kpe = jnp.stack(kpe_list, axis=1)
    return (xp.reshape(n_prompt, prompt_len, D_MODEL), xs.reshape(n_sample, sample_len, D_MODEL),
            new_ckv, new_kpe)
```

```python
import functools

import jax
import jax.numpy as jnp
import numpy as np
from jax import lax
from jax.experimental import pallas as pl
from jax.experimental.pallas import tpu as pltpu

D_MODEL = 1024
DEPTH = 2
GRID_W = 64
MLA_HEADS = 8
Q_LORA = 384
KV_LORA = 256
QK_NOPE = 64
QK_ROPE = 32
V_HEAD = 64
MLA_W = MLA_HEADS * V_HEAD
SOFTMAX_SCALE = (QK_NOPE + QK_ROPE) ** -0.5
ROPE_BASE = 10000.0
CONV_W = 256
CONV_K = 31
GMLP_W = 256
GMLP_GROUPS = 4
GMLP_CHUNK = 128
POOL_W = 256
POOL_GROUPS = 4
N_BRANCH = 4
D_FF = 2816
FFN_K = 3
NORM_EPS = 1e-6
OFF_KV = Q_LORA
OFF_KPE = OFF_KV + KV_LORA
OFF_CONV = OFF_KPE + QK_ROPE
OFF_GMLP = OFF_CONV + 2 * CONV_W
OFF_POOL = OFF_GMLP + 2 * GMLP_W
OFF_GATE = OFF_POOL + POOL_W

LANES = 128
SUBLANES = 8
VMEM_LIMIT_BYTES = 56 * 1024 * 1024

S_Q = 0
S_KV = S_Q + Q_LORA
S_CONV = S_KV + KV_LORA
S_GMLP = S_CONV + 2 * CONV_W
S_POOL = S_GMLP + 2 * GMLP_W
S_KPE = S_POOL + POOL_W
S_KPEP = S_KPE + LANES
S_TOTAL = S_KPEP + LANES

HEAD_SLOT = LANES
QK_CAT = MLA_HEADS * HEAD_SLOT

ROW_TILE = 512
FFN_TILE = 1024
FFN_CHUNK = 256
FFN_NCHUNK = D_FF // FFN_CHUNK
SEQ_ROWS = 64
PAD_ROWS = 16
ATTN_Q_TILE = 256
ATTN_GROUP = 4

BF16 = jnp.bfloat16
F32 = jnp.float32
INV_SQRT2 = 0.7071067811865476
LOG2E = 1.4426950408889634


def _dot(a, b):
    return jnp.dot(a, b, preferred_element_type=F32)


def _dot_nt(a, b):
    return lax.dot_general(a, b, (((1,), (1,)), ((), ())), preferred_element_type=F32)


def _rms(x, g):
    return x * lax.rsqrt(jnp.mean(x * x, axis=-1, keepdims=True) + NORM_EPS) * g


def _layernorm(x, g, b):
    mu = jnp.mean(x, axis=-1, keepdims=True)
    xc = x - mu
    var = jnp.mean(xc * xc, axis=-1, keepdims=True)
    return xc * lax.rsqrt(var + NORM_EPS) * g + b


def _sigmoid(x):
    return 1.0 / (1.0 + jnp.exp(-x))


def _const_spec(shape):
    nd = len(shape)
    return pl.BlockSpec(shape, lambda *_: (0,) * nd, pipeline_mode=pl.Buffered(1))


def _layer_spec(stacked, layer):
    tail = stacked.shape[1:]
    return pl.BlockSpec((None,) + tail, lambda *_: (layer,) + (0,) * len(tail), pipeline_mode=pl.Buffered(1))


def _params(n_axes, flags=None):
    return pltpu.CompilerParams(dimension_semantics=("arbitrary",) * n_axes,
                                vmem_limit_bytes=VMEM_LIMIT_BYTES, flags=flags)


def _mod_kernel(cv_ref, w_ref, b_ref, o_ref):
    cv = cv_ref[...]
    s = (cv * _sigmoid(cv)).astype(BF16)
    o_ref[0] = _dot(s, w_ref[0].astype(BF16)) + b_ref[0]


def _mod_call(cvec, ada_w, ada_b):
    tn = 1536
    n = 6 * D_MODEL
    return pl.pallas_call(
        _mod_kernel,
        grid=(DEPTH, n // tn),
        in_specs=[pl.BlockSpec((SUBLANES, D_MODEL), lambda l, j: (0, 0)),
                  pl.BlockSpec((1, D_MODEL, tn), lambda l, j: (l, 0, j)),
                  pl.BlockSpec((1, 1, tn), lambda l, j: (l, 0, j))],
        out_specs=pl.BlockSpec((1, SUBLANES, tn), lambda l, j: (l, 0, j)),
        out_shape=jax.ShapeDtypeStruct((DEPTH, SUBLANES, n), F32),
        compiler_params=_params(2),
        name="mod",
    )(cvec, ada_w, ada_b.reshape(DEPTH, 1, n))


def _rowproj_kernel(rope, x_ref, mod_ref, e_ref, preg_ref, ws_ref, qng_ref, wq_ref, kvng_ref, wkv_ref,
                    glng_ref, glnb_ref, gws_ref, gbm_ref, *rest):
    if rope:
        ccat_ref, scat_ref, ckpe_ref, skpe_ref = rest[:4]
        rest = rest[4:]
        h_ref, q_ref, qp_ref, k_ref, v_ref, ckv_ref, kpe_ref, a_ref, pin_ref, fc_ref = rest
    else:
        h_ref, q_ref, k_ref, v_ref, ckv_ref, kpe_ref, a_ref, pin_ref, fc_ref = rest
    x = x_ref[...]
    tm = x.shape[0]
    mod = mod_ref[...]
    sh1 = mod[:, 0:D_MODEL]
    sc1 = mod[:, D_MODEL:2 * D_MODEL]
    h = _rms(x, preg_ref[...]) * (1.0 + sc1) + sh1
    hb = h.astype(BF16)
    h_ref[...] = hb

    qa = _dot(hb, ws_ref[:, S_Q:S_KV])
    kvc = _dot(hb, ws_ref[:, S_KV:S_CONV])
    gm = _dot(hb, ws_ref[:, S_GMLP:S_POOL])
    ci = _dot(hb, ws_ref[:, S_CONV:S_GMLP])
    kpa = _dot(hb, ws_ref[:, S_KPE:S_KPEP])
    if rope:
        kpb = _dot(hb, ws_ref[:, S_KPEP:S_TOTAL])
    pin_ref[...] = _dot(hb, ws_ref[:, S_POOL:S_KPE])

    qn = _rms(qa, qng_ref[...]).astype(BF16)
    ckv = _rms(kvc, kvng_ref[...])
    ckv_ref[...] = ckv
    kpe_ref[...] = kpa[:, 0:QK_ROPE]
    kpr = kpa * ckpe_ref[...] + kpb * skpe_ref[...] if rope else kpa
    qq = _dot(qn, wq_ref[...]) * (SOFTMAX_SCALE * LOG2E)
    kvb = _dot(ckv.astype(BF16), wkv_ref[...])
    kpx = _dot(kpr.astype(BF16), e_ref[...])
    if rope:
        ccat = ccat_ref[...]
        scat = scat_ref[...]
        for hd in range(MLA_HEADS):
            lo = hd * HEAD_SLOT
            qpl = qq[:, lo:lo + HEAD_SLOT]
            qpr = qq[:, QK_CAT + lo:QK_CAT + lo + HEAD_SLOT]
            q_ref[:, lo:lo + HEAD_SLOT] = (qpl * ccat + qpr * scat).astype(BF16)
            qp_ref[:, lo:lo + HEAD_SLOT] = qpl.astype(BF16)
    else:
        q_ref[...] = qq.astype(BF16)
    k_ref[...] = (kvb[:, 0:QK_CAT] + kpx).astype(BF16)
    v_ref[...] = kvb[:, QK_CAT:QK_CAT + MLA_W].astype(BF16)

    a_ref[...] = ci[:, 0:CONV_W] * _sigmoid(ci[:, CONV_W:2 * CONV_W])

    uv = 0.5 * gm * (1.0 + lax.erf(gm * INV_SQRT2))
    u = uv[:, 0:GMLP_W]
    v = _layernorm(uv[:, GMLP_W:2 * GMLP_W], glng_ref[...], glnb_ref[...])
    grp = lax.broadcasted_iota(jnp.int32, (GMLP_CHUNK, GMLP_W), 1) // (GMLP_W // GMLP_GROUPS)
    gws = gws_ref[...]
    gbm = gbm_ref[...]
    for n in range(tm // GMLP_CHUNK):
        r0 = n * GMLP_CHUNK
        vc = v[r0:r0 + GMLP_CHUNK]
        vs = jnp.concatenate([jnp.where(grp == g, vc, 0.0) for g in range(GMLP_GROUPS)], axis=0)
        sv = _dot(gws, vs.astype(BF16)) + gbm
        fc_ref[r0:r0 + GMLP_CHUNK, :] = (u[r0:r0 + GMLP_CHUNK] * sv).astype(BF16)


def _rowproj_call(x2d, mod3, seq_len, mod_row_of_tile, lw, rope_tabs):
    rows = x2d.shape[0]
    tm = ROW_TILE
    rope = rope_tabs is not None
    names = ["pre_mix_g", "w_small", "q_norm_g", "wq_rope" if rope else "wq", "kv_norm_g", "wkv",
             "gmlp_ln_g", "gmlp_ln_b", "gmlp_wcat", "gmlp_bmat"]
    row_spec = lambda w: pl.BlockSpec((tm, w), lambda i: (i, 0))
    in_specs = [row_spec(D_MODEL),
                pl.BlockSpec((None, 1, 6 * D_MODEL), lambda i: (mod_row_of_tile(i), 0, 0)),
                _const_spec((LANES, QK_CAT))] + [_layer_spec(lw.stacked[n], lw.layer) for n in names]
    args = [x2d, mod3, lw.e_kpe] + [lw.stacked[n] for n in names]
    outs = [("h", D_MODEL, BF16), ("q", QK_CAT, BF16)]
    if rope:
        tiles_per_seq = seq_len // tm
        tab_spec = pl.BlockSpec((tm, LANES), lambda i: (i % tiles_per_seq, 0))
        in_specs += [tab_spec] * 4
        args += list(rope_tabs)
        outs.append(("qp", QK_CAT, BF16))
    outs += [("k", QK_CAT, BF16), ("v", MLA_W, BF16), ("ckv", KV_LORA, F32), ("kpe", QK_ROPE, F32),
             ("a", CONV_W, F32), ("pin", POOL_W, F32), ("fc", GMLP_W, BF16)]
    res = pl.pallas_call(
        functools.partial(_rowproj_kernel, rope),
        grid=(rows // tm,),
        in_specs=in_specs,
        out_specs=[row_spec(w) for _, w, _ in outs],
        out_shape=[jax.ShapeDtypeStruct((rows, w), dt) for _, w, dt in outs],
        compiler_params=_params(1),
        name="rowproj_rope" if rope else "rowproj",
    )(*args)
    return {name: r for (name, _, _), r in zip(outs, res)}


def _seqmix_kernel(seq_len, a_ref, pin_ref, cw_ref, cb_ref, clng_ref, clnb_ref, pw_ref, ps_ref,
                   fb_ref, fd_ref, apad_ref, ppad_ref, pooled_ref):
    L = seq_len
    zeros = jnp.zeros((PAD_ROWS, CONV_W), F32)
    for pad_ref, src_ref in ((apad_ref, a_ref), (ppad_ref, pin_ref)):
        pad_ref[0:PAD_ROWS, :] = zeros
        pad_ref[L + PAD_ROWS:L + 2 * PAD_ROWS, :] = zeros
        pad_ref[PAD_ROWS:L + PAD_ROWS, :] = src_ref[...]
    cb = cb_ref[...]
    clng = clng_ref[...]
    clnb = clnb_ref[...]
    half_conv = CONV_K // 2
    lane = lax.broadcasted_iota(jnp.int32, (SEQ_ROWS, POOL_W), 1)
    grp_w = POOL_W // POOL_GROUPS
    half = jnp.where(lane < grp_w, 1, jnp.where(lane < 2 * grp_w, 2, jnp.where(lane < 3 * grp_w, 4, 8)))
    win = SEQ_ROWS + 2 * PAD_ROWS

    def shifted_windows(pad_ref, r0):
        w = pad_ref[r0:r0 + win, :]
        return [w] + [pltpu.roll(w, win - b, 0) for b in range(1, SUBLANES)]

    def tap_of(xs, off):
        a = off // SUBLANES * SUBLANES
        return xs[off % SUBLANES][a:a + SEQ_ROWS]

    for r0 in range(0, L, SEQ_ROWS):
        xs = shifted_windows(apad_ref, r0)
        acc = jnp.zeros((SEQ_ROWS, CONV_W), F32) + cb
        for k in range(CONV_K):
            acc = acc + tap_of(xs, PAD_ROWS + k - half_conv) * cw_ref[pl.ds(k, 1), :]
        y = _layernorm(acc, clng, clnb)
        fb_ref[r0:r0 + SEQ_ROWS, :] = (y * _sigmoid(y)).astype(BF16)

        ps = shifted_windows(ppad_ref, r0)

        def tap(j):
            return tap_of(ps, PAD_ROWS + j)
        x0 = tap(0)
        s2 = tap(-1) + x0
        s4 = s2 + tap(-2) + tap(1)
        s8 = s4 + tap(-4) + tap(-3) + tap(2) + tap(3)
        s16 = s8 + tap(-8) + tap(-7) + tap(-6) + tap(-5) + tap(4) + tap(5) + tap(6) + tap(7)
        ssum = jnp.where(lane < grp_w, s2, jnp.where(lane < 2 * grp_w, s4, jnp.where(lane < 3 * grp_w, s8, s16)))
        t = r0 + lax.broadcasted_iota(jnp.int32, (SEQ_ROWS, POOL_W), 0)
        cnt = jnp.minimum(t + half, L) - jnp.maximum(t - half, 0)
        pooled_ref[r0:r0 + SEQ_ROWS, :] = ssum / cnt.astype(F32) - x0
    fd_ref[...] = (_dot(pooled_ref[...].astype(BF16), pw_ref[...]) * ps_ref[...]).astype(BF16)


def _seqmix_call(a2d, pin2d, seq_len, lw):
    rows = a2d.shape[0]
    L = seq_len
    names = ["conv_dw", "conv_dw_b", "conv_ln_g", "conv_ln_b", "pool_bd", "pool_scale"]
    seq_spec = pl.BlockSpec((L, CONV_W), lambda b: (b, 0))
    return pl.pallas_call(
        functools.partial(_seqmix_kernel, L),
        grid=(rows // L,),
        in_specs=[seq_spec, seq_spec] + [_layer_spec(lw.stacked[n], lw.layer) for n in names],
        out_specs=[seq_spec, seq_spec],
        out_shape=[jax.ShapeDtypeStruct((rows, CONV_W), BF16), jax.ShapeDtypeStruct((rows, POOL_W), BF16)],
        scratch_shapes=[pltpu.VMEM((L + 2 * PAD_ROWS, CONV_W), F32),
                        pltpu.VMEM((L + 2 * PAD_ROWS, POOL_W), F32),
                        pltpu.VMEM((L, POOL_W), F32)],
        compiler_params=_params(1),
        name="seqmix",
    )(a2d, pin2d, *[lw.stacked[n] for n in names])


def _ctxkv_kernel(ckv_ref, kpe_ref, wkv_ref, e_ref, k_ref, v_ref):
    kvb = _dot(ckv_ref[...].astype(BF16), wkv_ref[...])
    k_ref[...] = (kvb[:, 0:QK_CAT] + _dot(kpe_ref[...].astype(BF16), e_ref[...])).astype(BF16)
    v_ref[...] = kvb[:, QK_CAT:QK_CAT + MLA_W].astype(BF16)


def _ctxkv_call(ckv2d, kpe2d, lw):
    rows = ckv2d.shape[0]
    return pl.pallas_call(
        _ctxkv_kernel,
        grid=(1,),
        in_specs=[pl.BlockSpec((rows, KV_LORA), lambda i: (0, 0)),
                  pl.BlockSpec((rows, LANES), lambda i: (0, 0)),
                  _layer_spec(lw.stacked["wkv"], lw.layer), _const_spec((LANES, QK_CAT))],
        out_specs=[pl.BlockSpec((rows, QK_CAT), lambda i: (0, 0)), pl.BlockSpec((rows, MLA_W), lambda i: (0, 0))],
        out_shape=[jax.ShapeDtypeStruct((rows, QK_CAT), BF16), jax.ShapeDtypeStruct((rows, MLA_W), BF16)],
        compiler_params=_params(1),
        name="ctxkv",
    )(ckv2d, kpe2d, lw.stacked["wkv"], lw.e_kpe)


def _attn_kernel(nseg, group, q_rows, k_rows, *refs):
    q_refs = refs[0:nseg]
    k_refs = refs[nseg:2 * nseg]
    v_refs = refs[2 * nseg:3 * nseg]
    o_ref = refs[3 * nseg]
    out_lane = lax.broadcasted_iota(jnp.int32, (q_rows, LANES), 1)
    for g in range(group):
        qr = slice(g * q_rows, (g + 1) * q_rows)
        krs = [slice(g * n, (g + 1) * n) for n in k_rows]
        scores = [[_dot_nt(q_refs[s][qr, hd * HEAD_SLOT:(hd + 1) * HEAD_SLOT],
                           k_refs[s][krs[s], hd * HEAD_SLOT:(hd + 1) * HEAD_SLOT]) for s in range(nseg)]
                  for hd in range(MLA_HEADS)]
        probs = []
        for hd in range(MLA_HEADS):
            m = functools.reduce(jnp.maximum, [jnp.max(sc, axis=-1, keepdims=True) for sc in scores[hd]])
            probs.append([jnp.exp2(sc - m).astype(BF16) for sc in scores[hd]])
        for pair in range(MLA_HEADS // 2):
            plo = pair * LANES
            outs = []
            for hh in range(2):
                o = None
                for s in range(nseg):
                    vp = v_refs[s][krs[s], plo:plo + LANES]
                    lane = lax.broadcasted_iota(jnp.int32, vp.shape, 1)
                    own = (lane < V_HEAD) if hh == 0 else (lane >= V_HEAD)
                    t = _dot(probs[2 * pair + hh][s], jnp.where(own, vp, jnp.ones_like(vp)))
                    o = t if o is None else o + t
                outs.append(o * (1.0 / pltpu.roll(o, V_HEAD, 1)))
            o_ref[qr, plo:plo + LANES] = jnp.where(out_lane < V_HEAD, outs[0], outs[1]).astype(BF16)


def _attn_call(qs, ks, vs, seq_len):
    nseg = len(qs)
    rows = qs[0].shape[0]
    nb = rows // seq_len
    tq = min(ATTN_Q_TILE, seq_len)
    nq = seq_len // tq
    group = ATTN_GROUP if nq == 1 else 1
    k_rows = [k.shape[0] // nb for k in ks]
    q_spec = pl.BlockSpec((group * tq, QK_CAT), lambda b, j: (b * nq + j, 0))
    k_specs = [pl.BlockSpec((group * n, QK_CAT), lambda b, j: (b, 0)) for n in k_rows]
    v_specs = [pl.BlockSpec((group * n, MLA_W), lambda b, j: (b, 0)) for n in k_rows]
    return pl.pallas_call(
        functools.partial(_attn_kernel, nseg, group, tq, k_rows),
        grid=(nb // group, nq),
        in_specs=[q_spec] * nseg + k_specs + v_specs,
        out_specs=pl.BlockSpec((group * tq, MLA_W), lambda b, j: (b * nq + j, 0)),
        out_shape=jax.ShapeDtypeStruct((rows, MLA_W), BF16),
        compiler_params=_params(2),
        name="attn%d" % nseg,
    )(*qs, *ks, *vs)


def _merge_kernel(x_ref, mod_ref, h_ref, fa_ref, fb_ref, fc_ref, fd_ref, wg_ref, bg_ref,
                  wa_ref, wb_ref, wc_ref, wd_ref, wout_ref, postg_ref, preg_ref, x1_ref, h2_ref):
    hb = h_ref[...]
    acc = None
    for k, (f_ref, w_ref) in enumerate(((fa_ref, wa_ref), (fb_ref, wb_ref), (fc_ref, wc_ref), (fd_ref, wd_ref))):
        cols = slice(k * D_MODEL, (k + 1) * D_MODEL)
        gate = _sigmoid(_dot(hb, wg_ref[:, cols]) + bg_ref[:, cols])
        t = gate * _dot(f_ref[...], w_ref[...])
        acc = t if acc is None else acc + t
    mix = _dot(acc.astype(BF16), wout_ref[...])
    mod = mod_ref[...]
    g1 = mod[:, 2 * D_MODEL:3 * D_MODEL]
    sh2 = mod[:, 3 * D_MODEL:4 * D_MODEL]
    sc2 = mod[:, 4 * D_MODEL:5 * D_MODEL]
    x1 = x_ref[...] + g1 * _rms(mix, postg_ref[...])
    x1_ref[...] = x1
    h2_ref[...] = (_rms(x1, preg_ref[...]) * (1.0 + sc2) + sh2).astype(BF16)


def _merge_call(x2d, mod3, mod_row_of_tile, h, fa, fb, fc, fd, lw):
    rows = x2d.shape[0]
    tm = ROW_TILE
    names = ["w_gate", "b_gate", "w_o_mla", "w_conv_out", "w_gmlp_out", "w_pool_out", "w_out",
             "post_mix_g", "pre_ffn_g"]
    row_spec = lambda w: pl.BlockSpec((tm, w), lambda i: (i, 0))
    return pl.pallas_call(
        _merge_kernel,
        grid=(rows // tm,),
        in_specs=[row_spec(D_MODEL),
                  pl.BlockSpec((None, 1, 6 * D_MODEL), lambda i: (mod_row_of_tile(i), 0, 0)),
                  row_spec(D_MODEL), row_spec(MLA_W), row_spec(CONV_W), row_spec(GMLP_W), row_spec(POOL_W)]
                 + [_layer_spec(lw.stacked[n], lw.layer) for n in names],
        out_specs=[row_spec(D_MODEL), row_spec(D_MODEL)],
        out_shape=[jax.ShapeDtypeStruct((rows, D_MODEL), F32), jax.ShapeDtypeStruct((rows, D_MODEL), BF16)],
        compiler_params=_params(1),
        name="merge",
    )(x2d, mod3, h, fa, fb, fc, fd, *[lw.stacked[n] for n in names])


def _ffn_kernel(seq_len, h2_ref, x1_ref, mod_ref, wup_ref, dw_ref, dwb_ref, wdn_ref, postg_ref, o_ref, act_ref):
    tm = h2_ref.shape[0]
    assert seq_len & (seq_len - 1) == 0
    pos = lax.broadcasted_iota(jnp.int32, (tm, FFN_CHUNK), 0) & (seq_len - 1)
    first = pos == 0
    last = pos == seq_len - 1

    def conv_cols(c0):
        cols = slice(c0, c0 + FFN_CHUNK)
        up = _dot(h2_ref[...], wup_ref[:, cols])
        dw = dw_ref[:, cols]
        prev = jnp.where(first, 0.0, pltpu.roll(up, 1, 0))
        nxt = jnp.where(last, 0.0, pltpu.roll(up, tm - 1, 0))
        return prev * dw[0:1, :] + up * dw[1:2, :] + nxt * dw[2:3, :] + dwb_ref[:, cols]

    for c in range(FFN_NCHUNK):
        c0 = c * FFN_CHUNK
        g = conv_cols(c0)
        val = conv_cols(D_FF + c0)
        act_ref[:, c0:c0 + FFN_CHUNK] = (g * _sigmoid(g) * val).astype(BF16)
    f = _dot(act_ref[...], wdn_ref[...])
    g2 = mod_ref[...][:, 5 * D_MODEL:6 * D_MODEL]
    o_ref[...] = x1_ref[...] + g2 * _rms(f, postg_ref[...])


def _ffn_call(h2, x1, mod3, seq_len, mod_row_of_tile, lw):
    rows = h2.shape[0]
    tm = FFN_TILE
    names = ["ffn_up", "ffn_dw", "ffn_dw_b", "ffn_down", "post_ffn_g"]
    row_spec = pl.BlockSpec((tm, D_MODEL), lambda i: (i, 0))
    return pl.pallas_call(
        functools.partial(_ffn_kernel, seq_len),
        grid=(rows // tm,),
        in_specs=[row_spec, row_spec,
                  pl.BlockSpec((None, 1, 6 * D_MODEL), lambda i: (mod_row_of_tile(i), 0, 0))]
                 + [_layer_spec(lw.stacked[n], lw.layer) for n in names],
        out_specs=row_spec,
        out_shape=jax.ShapeDtypeStruct((rows, D_MODEL), F32),
        scratch_shapes=[pltpu.VMEM((tm, D_FF), BF16)],
        compiler_params=_params(1),
        name="ffn",
    )(h2, x1, mod3, *[lw.stacked[n] for n in names])


def _rot_half_perm():
    src = np.zeros((QK_ROPE,), np.int32)
    sign = np.zeros((QK_ROPE,), np.float32)
    quarter = QK_ROPE // 4
    for d in range(QK_ROPE):
        e = d % (2 * quarter)
        if e < quarter:
            src[d], sign[d] = d + quarter, -1.0
        else:
            src[d], sign[d] = d - quarter, 1.0
    return src, sign


def _rope_tables(seq_len):
    quarter = QK_ROPE // 4
    inv = ROPE_BASE ** (-jnp.arange(quarter, dtype=F32) / quarter)
    t = jnp.arange(seq_len)
    row = (t // GRID_W).astype(F32)[:, None] * inv[None, :]
    col = (t % GRID_W).astype(F32)[:, None] * inv[None, :]
    ang = jnp.concatenate([row, row, col, col], axis=1)
    cos, sin = jnp.cos(ang), jnp.sin(ang)
    ones = jnp.ones((seq_len, QK_NOPE), F32)
    zeros_n = jnp.zeros((seq_len, QK_NOPE), F32)
    zeros_t = jnp.zeros((seq_len, HEAD_SLOT - QK_NOPE - QK_ROPE), F32)
    ccat = jnp.concatenate([ones, cos, zeros_t], axis=1)
    scat = jnp.concatenate([zeros_n, sin, zeros_t], axis=1)
    zeros_k = jnp.zeros((seq_len, LANES - QK_ROPE), F32)
    ckpe = jnp.concatenate([cos, zeros_k], axis=1)
    skpe = jnp.concatenate([sin, zeros_k], axis=1)
    return ccat, scat, ckpe, skpe


class _LayerWeights:
    def __init__(self, stacked, e_kpe, layer):
        self.stacked = stacked
        self.e_kpe = e_kpe
        self.layer = layer


def _prep_weights(W):
    src, sign = _rot_half_perm()
    depth = W["w_in"].shape[0]
    w_in = W["w_in"]
    kpe_w = w_in[:, :, OFF_KPE:OFF_CONV]
    kpe_wp = kpe_w[:, :, src] * sign
    zpad = jnp.zeros((depth, D_MODEL, LANES - QK_ROPE), F32)
    w_small = jnp.concatenate([w_in[:, :, 0:OFF_KPE], w_in[:, :, OFF_CONV:OFF_GATE], kpe_w, zpad, kpe_wp, zpad],
                              axis=2).astype(BF16)
    w_gate = w_in[:, :, OFF_GATE:].astype(BF16)

    wq = W["w_q_b"].reshape(depth, Q_LORA, MLA_HEADS, QK_NOPE + QK_ROPE)
    q_nope, q_pe = wq[..., :QK_NOPE], wq[..., QK_NOPE:]
    tail = jnp.zeros((depth, Q_LORA, MLA_HEADS, HEAD_SLOT - QK_NOPE - QK_ROPE), F32)
    wq_cat = jnp.concatenate([q_nope, q_pe, tail], axis=3).reshape(depth, Q_LORA, QK_CAT)
    wq_rot = jnp.concatenate([jnp.zeros_like(q_nope), q_pe[..., src] * sign, tail], axis=3)
    wq_rope = jnp.concatenate([wq_cat, wq_rot.reshape(depth, Q_LORA, QK_CAT)], axis=2)

    wkv = W["w_kv_b"].reshape(depth, KV_LORA, MLA_HEADS, QK_NOPE + V_HEAD)
    k_nope, v_w = wkv[..., :QK_NOPE], wkv[..., QK_NOPE:]
    k_cat = jnp.concatenate([k_nope, jnp.zeros((depth, KV_LORA, MLA_HEADS, HEAD_SLOT - QK_NOPE), F32)], axis=3)
    wkv_cat = jnp.concatenate([k_cat.reshape(depth, KV_LORA, QK_CAT), v_w.reshape(depth, KV_LORA, MLA_W)], axis=2)

    e = np.zeros((LANES, QK_CAT), np.float32)
    for hd in range(MLA_HEADS):
        for d in range(QK_ROPE):
            e[d, hd * HEAD_SLOT + QK_NOPE + d] = 1.0

    eye = jnp.eye(POOL_GROUPS, dtype=F32)
    pool_bd = (eye[None, :, None, :, None] * W["pool_w"][:, :, :, None, :]).reshape(depth, POOL_W, POOL_W)

    row = lambda a: a.reshape(depth, 1, -1)
    stacked = {
        "pre_mix_g": row(W["pre_mix_g"]), "post_mix_g": row(W["post_mix_g"]),
        "pre_ffn_g": row(W["pre_ffn_g"]), "post_ffn_g": row(W["post_ffn_g"]),
        "w_small": w_small, "w_gate": w_gate, "b_gate": row(W["b_gate"]),
        "q_norm_g": row(W["q_norm_g"]), "wq": wq_cat.astype(BF16), "wq_rope": wq_rope.astype(BF16),
        "kv_norm_g": row(W["kv_norm_g"]), "wkv": wkv_cat.astype(BF16),
        "w_o_mla": W["w_o_mla"].astype(BF16),
        "conv_dw": W["conv_dw"], "conv_dw_b": row(W["conv_dw_b"]),
        "conv_ln_g": row(W["conv_ln_g"]), "conv_ln_b": row(W["conv_ln_b"]),
        "w_conv_out": W["w_conv_out"].astype(BF16),
        "gmlp_ln_g": row(W["gmlp_ln_g"]), "gmlp_ln_b": row(W["gmlp_ln_b"]),
        "gmlp_wcat": W["gmlp_ws"].transpose(0, 2, 1, 3).reshape(depth, GMLP_CHUNK, GMLP_GROUPS * GMLP_CHUNK).astype(BF16),
        "gmlp_bmat": jnp.repeat(W["gmlp_bs"].transpose(0, 2, 1), GMLP_W // GMLP_GROUPS, axis=2),
        "w_gmlp_out": W["w_gmlp_out"].astype(BF16),
        "pool_bd": pool_bd.astype(BF16),
        "pool_scale": row(W["pool_scale"]),
        "w_pool_out": W["w_pool_out"].astype(BF16),
        "w_out": W["w_out"].astype(BF16),
        "ffn_up": W["ffn_up"].astype(BF16),
        "ffn_dw": W["ffn_dw"],
        "ffn_dw_b": row(W["ffn_dw_b"]),
        "ffn_down": W["ffn_down"].astype(BF16),
    }
    return stacked, jnp.asarray(e, BF16)


def _trunk_layer(x2d, seq_len, mod3, mod_row_of_tile_small, mod_row_of_tile_ffn, lw, rope_tabs, ctx):
    p = _rowproj_call(x2d, mod3, seq_len, mod_row_of_tile_small, lw, rope_tabs)
    fb, fd = _seqmix_call(p["a"], p["pin"], seq_len, lw)
    if ctx is None:
        fa = _attn_call([p["q"]], [p["k"]], [p["v"]], seq_len)
    else:
        k_ctx, v_ctx = ctx
        fa = _attn_call([p["q"], p["qp"]], [p["k"], k_ctx], [p["v"], v_ctx], seq_len)
    x1, h2 = _merge_call(x2d, mod3, mod_row_of_tile_small, p["h"], fa, fb, p["fc"], fd, lw)
    x2 = _ffn_call(h2, x1, mod3, seq_len, mod_row_of_tile_ffn, lw)
    return x2, p["ckv"], p["kpe"]


def kernel(x_prompt, x_sample, cache_ckv, cache_kpe, c, c_ctx, ada_w, ada_b, pre_mix_g, post_mix_g, pre_ffn_g, post_ffn_g, w_in, b_gate, q_norm_g, w_q_b, kv_norm_g, w_kv_b, w_o_mla, conv_dw, conv_dw_b, conv_ln_g, conv_ln_b, w_conv_out, gmlp_ln_g, gmlp_ln_b, gmlp_ws, gmlp_bs, w_gmlp_out, pool_w, pool_scale, w_pool_out, w_out, ffn_up, ffn_dw, ffn_dw_b, ffn_down):
    W = dict(pre_mix_g=pre_mix_g, post_mix_g=post_mix_g, pre_ffn_g=pre_ffn_g, post_ffn_g=post_ffn_g,
             w_in=w_in, b_gate=b_gate, q_norm_g=q_norm_g, w_q_b=w_q_b, kv_norm_g=kv_norm_g,
             w_kv_b=w_kv_b, w_o_mla=w_o_mla, conv_dw=conv_dw, conv_dw_b=conv_dw_b,
             conv_ln_g=conv_ln_g, conv_ln_b=conv_ln_b, w_conv_out=w_conv_out, gmlp_ln_g=gmlp_ln_g,
             gmlp_ln_b=gmlp_ln_b, gmlp_ws=gmlp_ws, gmlp_bs=gmlp_bs, w_gmlp_out=w_gmlp_out,
             pool_w=pool_w, pool_scale=pool_scale, w_pool_out=w_pool_out, w_out=w_out,
             ffn_up=ffn_up, ffn_dw=ffn_dw, ffn_dw_b=ffn_dw_b, ffn_down=ffn_down)
    n_prompt, prompt_len, _ = x_prompt.shape
    n_sample, sample_len, _ = x_sample.shape
    past_len = cache_ckv.shape[2]
    ctx_row = n_sample

    cvec = jnp.zeros((SUBLANES, D_MODEL), F32).at[0:n_sample].set(c).at[ctx_row].set(c_ctx)
    mod = _mod_call(cvec, ada_w, ada_b)

    rope_tabs = _rope_tables(sample_len)
    xp = x_prompt.reshape(n_prompt * prompt_len, D_MODEL)
    xs = x_sample.reshape(n_sample * sample_len, D_MODEL)
    ckv_list, kpe_list = [], []
    stacked, e_kpe = _prep_weights(W)
    for l in range(DEPTH):
        mod3 = mod[l].reshape(SUBLANES, 1, 6 * D_MODEL)
        lw = _LayerWeights(stacked, e_kpe, l)
        xp, ckv_l, kpe_l = _trunk_layer(xp, prompt_len, mod3, lambda i: ctx_row, lambda i: ctx_row,
                                        lw, None, None)
        ckv_list.append(ckv_l.reshape(n_prompt, prompt_len, KV_LORA))
        kpe_list.append(kpe_l.reshape(n_prompt, prompt_len, QK_ROPE))
        kpe_pad = jnp.pad(cache_kpe[:, l].reshape(n_sample * past_len, QK_ROPE),
                          ((0, 0), (0, LANES - QK_ROPE)))
        ctx = _ctxkv_call(cache_ckv[:, l].reshape(n_sample * past_len, KV_LORA), kpe_pad, lw)
        xs, _, _ = _trunk_layer(xs, sample_len, mod3,
                                lambda i: (i * ROW_TILE) // sample_len,
                                lambda i: (i * FFN_TILE) // sample_len,
                                lw, rope_tabs, ctx)
    new_ckv = jnp.stack(ckv_list, axis=1)
    new_kpe = jnp.stack(kpe_list, axis=1)
    return (xp.reshape(n_prompt, prompt_len, D_MODEL), xs.reshape(n_sample, sample_len, D_MODEL),
            new_ckv, new_kpe)
```

```python
import functools

import jax
import jax.numpy as jnp
import numpy as np
from jax import lax
from jax.experimental import pallas as pl
from jax.experimental.pallas import tpu as pltpu

D_MODEL = 1024
DEPTH = 2
GRID_W = 64
MLA_HEADS = 8
Q_LORA = 384
KV_LORA = 256
QK_NOPE = 64
QK_ROPE = 32
V_HEAD = 64
MLA_W = MLA_HEADS * V_HEAD
SOFTMAX_SCALE = (QK_NOPE + QK_ROPE) ** -0.5
ROPE_BASE = 10000.0
CONV_W = 256
CONV_K = 31
GMLP_W = 256
GMLP_GROUPS = 4
GMLP_CHUNK = 128
POOL_W = 256
POOL_GROUPS = 4
N_BRANCH = 4
D_FF = 2816
FFN_K = 3
NORM_EPS = 1e-6
OFF_KV = Q_LORA
OFF_KPE = OFF_KV + KV_LORA
OFF_CONV = OFF_KPE + QK_ROPE
OFF_GMLP = OFF_CONV + 2 * CONV_W
OFF_POOL = OFF_GMLP + 2 * GMLP_W
OFF_GATE = OFF_POOL + POOL_W

LANES = 128
SUBLANES = 8
VMEM_LIMIT_BYTES = 56 * 1024 * 1024

S_Q = 0
S_KV = S_Q + Q_LORA
S_CONV = S_KV + KV_LORA
S_GMLP = S_CONV + 2 * CONV_W
S_POOL = S_GMLP + 2 * GMLP_W
S_KPE = S_POOL + POOL_W
S_KPEP = S_KPE + LANES
S_TOTAL = S_KPEP + LANES

HEAD_SLOT = LANES
QK_CAT = MLA_HEADS * HEAD_SLOT

ROW_TILE = 512
MERGE_ROWS = 256
FFN_TILE = 1024
FFN_CHUNK = 256
FFN_NCHUNK = D_FF // FFN_CHUNK
SEQ_ROWS = 64
PAD_ROWS = 16
ATTN_Q_TILE = 256
ATTN_GROUP = 4

BF16 = jnp.bfloat16
F32 = jnp.float32
INV_SQRT2 = 0.7071067811865476
LOG2E = 1.4426950408889634


def _dot(a, b):
    return jnp.dot(a, b, preferred_element_type=F32)


def _dot_nt(a, b):
    return lax.dot_general(a, b, (((1,), (1,)), ((), ())), preferred_element_type=F32)


def _rms(x, g):
    return x * lax.rsqrt(jnp.mean(x * x, axis=-1, keepdims=True) + NORM_EPS) * g


def _layernorm(x, g, b):
    mu = jnp.mean(x, axis=-1, keepdims=True)
    xc = x - mu
    var = jnp.mean(xc * xc, axis=-1, keepdims=True)
    return xc * lax.rsqrt(var + NORM_EPS) * g + b


def _sigmoid(x):
    return 1.0 / (1.0 + jnp.exp(-x))


def _const_spec(shape):
    nd = len(shape)
    return pl.BlockSpec(shape, lambda *_: (0,) * nd, pipeline_mode=pl.Buffered(1))


def _layer_spec(stacked, layer):
    tail = stacked.shape[1:]
    return pl.BlockSpec((None,) + tail, lambda *_: (layer,) + (0,) * len(tail), pipeline_mode=pl.Buffered(1))


def _params(n_axes, flags=None):
    return pltpu.CompilerParams(dimension_semantics=("arbitrary",) * n_axes,
                                vmem_limit_bytes=VMEM_LIMIT_BYTES, flags=flags)


def _mod_kernel(cv_ref, w_ref, b_ref, o_ref):
    cv = cv_ref[...]
    s = (cv * _sigmoid(cv)).astype(BF16)
    o_ref[0] = _dot(s, w_ref[0].astype(BF16)) + b_ref[0]


def _mod_call(cvec, ada_w, ada_b):
    tn = 1536
    n = 6 * D_MODEL
    return pl.pallas_call(
        _mod_kernel,
        grid=(DEPTH, n // tn),
        in_specs=[pl.BlockSpec((SUBLANES, D_MODEL), lambda l, j: (0, 0)),
                  pl.BlockSpec((1, D_MODEL, tn), lambda l, j: (l, 0, j)),
                  pl.BlockSpec((1, 1, tn), lambda l, j: (l, 0, j))],
        out_specs=pl.BlockSpec((1, SUBLANES, tn), lambda l, j: (l, 0, j)),
        out_shape=jax.ShapeDtypeStruct((DEPTH, SUBLANES, n), F32),
        compiler_params=_params(2),
        name="mod",
    )(cvec, ada_w, ada_b.reshape(DEPTH, 1, n))


def _rowproj_kernel(rope, x_ref, mod_ref, e_ref, preg_ref, ws_ref, qng_ref, wq_ref, kvng_ref, wkv_ref,
                    glng_ref, glnb_ref, gws_ref, gbm_ref, *rest):
    if rope:
        ccat_ref, scat_ref, ckpe_ref, skpe_ref = rest[:4]
        rest = rest[4:]
        h_ref, q_ref, qp_ref, k_ref, v_ref, ckv_ref, kpe_ref, a_ref, pin_ref, fc_ref = rest
    else:
        h_ref, q_ref, k_ref, v_ref, ckv_ref, kpe_ref, a_ref, pin_ref, fc_ref = rest
    x = x_ref[...]
    tm = x.shape[0]
    mod = mod_ref[...]
    sh1 = mod[:, 0:D_MODEL]
    sc1 = mod[:, D_MODEL:2 * D_MODEL]
    h = _rms(x, preg_ref[...]) * (1.0 + sc1) + sh1
    hb = h.astype(BF16)
    h_ref[...] = hb

    qa = _dot(hb, ws_ref[:, S_Q:S_KV])
    kvc = _dot(hb, ws_ref[:, S_KV:S_CONV])
    gm = _dot(hb, ws_ref[:, S_GMLP:S_POOL])
    ci = _dot(hb, ws_ref[:, S_CONV:S_GMLP])
    kpa = _dot(hb, ws_ref[:, S_KPE:S_KPEP])
    if rope:
        kpb = _dot(hb, ws_ref[:, S_KPEP:S_TOTAL])
    pin_ref[...] = _dot(hb, ws_ref[:, S_POOL:S_KPE])

    qn = _rms(qa, qng_ref[...]).astype(BF16)
    ckv = _rms(kvc, kvng_ref[...])
    ckv_ref[...] = ckv
    kpe_ref[...] = kpa[:, 0:QK_ROPE]
    kpr = kpa * ckpe_ref[...] + kpb * skpe_ref[...] if rope else kpa
    qq = _dot(qn, wq_ref[...]) * (SOFTMAX_SCALE * LOG2E)
    kvb = _dot(ckv.astype(BF16), wkv_ref[...])
    kpx = _dot(kpr.astype(BF16), e_ref[...])
    if rope:
        ccat = ccat_ref[...]
        scat = scat_ref[...]
        for hd in range(MLA_HEADS):
            lo = hd * HEAD_SLOT
            qpl = qq[:, lo:lo + HEAD_SLOT]
            qpr = qq[:, QK_CAT + lo:QK_CAT + lo + HEAD_SLOT]
            q_ref[:, lo:lo + HEAD_SLOT] = (qpl * ccat + qpr * scat).astype(BF16)
            qp_ref[:, lo:lo + HEAD_SLOT] = qpl.astype(BF16)
    else:
        q_ref[...] = qq.astype(BF16)
    k_ref[...] = (kvb[:, 0:QK_CAT] + kpx).astype(BF16)
    v_ref[...] = kvb[:, QK_CAT:QK_CAT + MLA_W].astype(BF16)

    a_ref[...] = ci[:, 0:CONV_W] * _sigmoid(ci[:, CONV_W:2 * CONV_W])

    uv = 0.5 * gm * (1.0 + lax.erf(gm * INV_SQRT2))
    u = uv[:, 0:GMLP_W]
    v = _layernorm(uv[:, GMLP_W:2 * GMLP_W], glng_ref[...], glnb_ref[...])
    grp = lax.broadcasted_iota(jnp.int32, (GMLP_CHUNK, GMLP_W), 1) // (GMLP_W // GMLP_GROUPS)
    gws = gws_ref[...]
    gbm = gbm_ref[...]
    for n in range(tm // GMLP_CHUNK):
        r0 = n * GMLP_CHUNK
        vc = v[r0:r0 + GMLP_CHUNK]
        vs = jnp.concatenate([jnp.where(grp == g, vc, 0.0) for g in range(GMLP_GROUPS)], axis=0)
        sv = _dot(gws, vs.astype(BF16)) + gbm
        fc_ref[r0:r0 + GMLP_CHUNK, :] = (u[r0:r0 + GMLP_CHUNK] * sv).astype(BF16)


def _rowproj_call(x2d, mod3, seq_len, mod_row_of_tile, lw, rope_tabs):
    rows = x2d.shape[0]
    tm = ROW_TILE
    rope = rope_tabs is not None
    names = ["pre_mix_g", "w_small", "q_norm_g", "wq_rope" if rope else "wq", "kv_norm_g", "wkv",
             "gmlp_ln_g", "gmlp_ln_b", "gmlp_wcat", "gmlp_bmat"]
    row_spec = lambda w: pl.BlockSpec((tm, w), lambda i: (i, 0))
    in_specs = [row_spec(D_MODEL),
                pl.BlockSpec((None, 1, 6 * D_MODEL), lambda i: (mod_row_of_tile(i), 0, 0)),
                _const_spec((LANES, QK_CAT))] + [_layer_spec(lw.stacked[n], lw.layer) for n in names]
    args = [x2d, mod3, lw.e_kpe] + [lw.stacked[n] for n in names]
    outs = [("h", D_MODEL, BF16), ("q", QK_CAT, BF16)]
    if rope:
        tiles_per_seq = seq_len // tm
        tab_spec = pl.BlockSpec((tm, LANES), lambda i: (i % tiles_per_seq, 0))
        in_specs += [tab_spec] * 4
        args += list(rope_tabs)
        outs.append(("qp", QK_CAT, BF16))
    outs += [("k", QK_CAT, BF16), ("v", MLA_W, BF16), ("ckv", KV_LORA, F32), ("kpe", QK_ROPE, F32),
             ("a", CONV_W, F32), ("pin", POOL_W, F32), ("fc", GMLP_W, BF16)]
    res = pl.pallas_call(
        functools.partial(_rowproj_kernel, rope),
        grid=(rows // tm,),
        in_specs=in_specs,
        out_specs=[row_spec(w) for _, w, _ in outs],
        out_shape=[jax.ShapeDtypeStruct((rows, w), dt) for _, w, dt in outs],
        compiler_params=_params(1),
        name="rowproj_rope" if rope else "rowproj",
    )(*args)
    return {name: r for (name, _, _), r in zip(outs, res)}


def _seqmix_kernel(seq_len, a_ref, pin_ref, cw_ref, cb_ref, clng_ref, clnb_ref, pw_ref, ps_ref,
                   fb_ref, fd_ref, apad_ref, ppad_ref, pooled_ref):
    L = seq_len
    zeros = jnp.zeros((PAD_ROWS, CONV_W), F32)
    for pad_ref, src_ref in ((apad_ref, a_ref), (ppad_ref, pin_ref)):
        pad_ref[0:PAD_ROWS, :] = zeros
        pad_ref[L + PAD_ROWS:L + 2 * PAD_ROWS, :] = zeros
        pad_ref[PAD_ROWS:L + PAD_ROWS, :] = src_ref[...]
    cb = cb_ref[...]
    clng = clng_ref[...]
    clnb = clnb_ref[...]
    half_conv = CONV_K // 2
    lane = lax.broadcasted_iota(jnp.int32, (SEQ_ROWS, POOL_W), 1)
    grp_w = POOL_W // POOL_GROUPS
    half = jnp.where(lane < grp_w, 1, jnp.where(lane < 2 * grp_w, 2, jnp.where(lane < 3 * grp_w, 4, 8)))
    win = SEQ_ROWS + 2 * PAD_ROWS

    def shifted_windows(pad_ref, r0):
        w = pad_ref[r0:r0 + win, :]
        return [w] + [pltpu.roll(w, win - b, 0) for b in range(1, SUBLANES)]

    def tap_of(xs, off):
        a = off // SUBLANES * SUBLANES
        return xs[off % SUBLANES][a:a + SEQ_ROWS]

    for r0 in range(0, L, SEQ_ROWS):
        xs = shifted_windows(apad_ref, r0)
        acc = jnp.zeros((SEQ_ROWS, CONV_W), F32) + cb
        for k in range(CONV_K):
            acc = acc + tap_of(xs, PAD_ROWS + k - half_conv) * cw_ref[pl.ds(k, 1), :]
        y = _layernorm(acc, clng, clnb)
        fb_ref[r0:r0 + SEQ_ROWS, :] = (y * _sigmoid(y)).astype(BF16)

        ps = shifted_windows(ppad_ref, r0)

        def tap(j):
            return tap_of(ps, PAD_ROWS + j)
        x0 = tap(0)
        s2 = tap(-1) + x0
        s4 = s2 + tap(-2) + tap(1)
        s8 = s4 + tap(-4) + tap(-3) + tap(2) + tap(3)
        s16 = s8 + tap(-8) + tap(-7) + tap(-6) + tap(-5) + tap(4) + tap(5) + tap(6) + tap(7)
        ssum = jnp.where(lane < grp_w, s2, jnp.where(lane < 2 * grp_w, s4, jnp.where(lane < 3 * grp_w, s8, s16)))
        t = r0 + lax.broadcasted_iota(jnp.int32, (SEQ_ROWS, POOL_W), 0)
        cnt = jnp.minimum(t + half, L) - jnp.maximum(t - half, 0)
        pooled_ref[r0:r0 + SEQ_ROWS, :] = ssum / cnt.astype(F32) - x0
    fd_ref[...] = (_dot(pooled_ref[...].astype(BF16), pw_ref[...]) * ps_ref[...]).astype(BF16)


def _seqmix_call(a2d, pin2d, seq_len, lw):
    rows = a2d.shape[0]
    L = seq_len
    names = ["conv_dw", "conv_dw_b", "conv_ln_g", "conv_ln_b", "pool_bd", "pool_scale"]
    seq_spec = pl.BlockSpec((L, CONV_W), lambda b: (b, 0))
    return pl.pallas_call(
        functools.partial(_seqmix_kernel, L),
        grid=(rows // L,),
        in_specs=[seq_spec, seq_spec] + [_layer_spec(lw.stacked[n], lw.layer) for n in names],
        out_specs=[seq_spec, seq_spec],
        out_shape=[jax.ShapeDtypeStruct((rows, CONV_W), BF16), jax.ShapeDtypeStruct((rows, POOL_W), BF16)],
        scratch_shapes=[pltpu.VMEM((L + 2 * PAD_ROWS, CONV_W), F32),
                        pltpu.VMEM((L + 2 * PAD_ROWS, POOL_W), F32),
                        pltpu.VMEM((L, POOL_W), F32)],
        compiler_params=_params(1),
        name="seqmix",
    )(a2d, pin2d, *[lw.stacked[n] for n in names])


def _ctxkv_kernel(ckv_ref, kpe_ref, wkv_ref, e_ref, k_ref, v_ref):
    kvb = _dot(ckv_ref[...].astype(BF16), wkv_ref[...])
    k_ref[...] = (kvb[:, 0:QK_CAT] + _dot(kpe_ref[...].astype(BF16), e_ref[...])).astype(BF16)
    v_ref[...] = kvb[:, QK_CAT:QK_CAT + MLA_W].astype(BF16)


def _ctxkv_call(ckv2d, kpe2d, lw):
    rows = ckv2d.shape[0]
    return pl.pallas_call(
        _ctxkv_kernel,
        grid=(1,),
        in_specs=[pl.BlockSpec((rows, KV_LORA), lambda i: (0, 0)),
                  pl.BlockSpec((rows, LANES), lambda i: (0, 0)),
                  _layer_spec(lw.stacked["wkv"], lw.layer), _const_spec((LANES, QK_CAT))],
        out_specs=[pl.BlockSpec((rows, QK_CAT), lambda i: (0, 0)), pl.BlockSpec((rows, MLA_W), lambda i: (0, 0))],
        out_shape=[jax.ShapeDtypeStruct((rows, QK_CAT), BF16), jax.ShapeDtypeStruct((rows, MLA_W), BF16)],
        compiler_params=_params(1),
        name="ctxkv",
    )(ckv2d, kpe2d, lw.stacked["wkv"], lw.e_kpe)


def _attn_kernel(nseg, group, q_rows, k_rows, *refs):
    q_refs = refs[0:nseg]
    k_refs = refs[nseg:2 * nseg]
    v_refs = refs[2 * nseg:3 * nseg]
    o_ref = refs[3 * nseg]
    out_lane = lax.broadcasted_iota(jnp.int32, (q_rows, LANES), 1)
    for g in range(group):
        qr = slice(g * q_rows, (g + 1) * q_rows)
        krs = [slice(g * n, (g + 1) * n) for n in k_rows]
        scores = [[_dot_nt(q_refs[s][qr, hd * HEAD_SLOT:(hd + 1) * HEAD_SLOT],
                           k_refs[s][krs[s], hd * HEAD_SLOT:(hd + 1) * HEAD_SLOT]) for s in range(nseg)]
                  for hd in range(MLA_HEADS)]
        probs = []
        for hd in range(MLA_HEADS):
            m = functools.reduce(jnp.maximum, [jnp.max(sc, axis=-1, keepdims=True) for sc in scores[hd]])
            probs.append([jnp.exp2(sc - m).astype(BF16) for sc in scores[hd]])
        for pair in range(MLA_HEADS // 2):
            plo = pair * LANES
            outs = []
            for hh in range(2):
                o = None
                for s in range(nseg):
                    vp = v_refs[s][krs[s], plo:plo + LANES]
                    lane = lax.broadcasted_iota(jnp.int32, vp.shape, 1)
                    own = (lane < V_HEAD) if hh == 0 else (lane >= V_HEAD)
                    t = _dot(probs[2 * pair + hh][s], jnp.where(own, vp, jnp.ones_like(vp)))
                    o = t if o is None else o + t
                outs.append(o * (1.0 / pltpu.roll(o, V_HEAD, 1)))
            o_ref[qr, plo:plo + LANES] = jnp.where(out_lane < V_HEAD, outs[0], outs[1]).astype(BF16)


def _attn_call(qs, ks, vs, seq_len):
    nseg = len(qs)
    rows = qs[0].shape[0]
    nb = rows // seq_len
    tq = min(ATTN_Q_TILE, seq_len)
    nq = seq_len // tq
    group = ATTN_GROUP if nq == 1 else 1
    k_rows = [k.shape[0] // nb for k in ks]
    q_spec = pl.BlockSpec((group * tq, QK_CAT), lambda b, j: (b * nq + j, 0))
    k_specs = [pl.BlockSpec((group * n, QK_CAT), lambda b, j: (b, 0)) for n in k_rows]
    v_specs = [pl.BlockSpec((group * n, MLA_W), lambda b, j: (b, 0)) for n in k_rows]
    return pl.pallas_call(
        functools.partial(_attn_kernel, nseg, group, tq, k_rows),
        grid=(nb // group, nq),
        in_specs=[q_spec] * nseg + k_specs + v_specs,
        out_specs=pl.BlockSpec((group * tq, MLA_W), lambda b, j: (b * nq + j, 0)),
        out_shape=jax.ShapeDtypeStruct((rows, MLA_W), BF16),
        compiler_params=_params(2),
        name="attn%d" % nseg,
    )(*qs, *ks, *vs)


def _merge_kernel(x_ref, mod_ref, h_ref, fa_ref, fb_ref, fc_ref, fd_ref, wg_ref, bg_ref,
                  wa_ref, wb_ref, wc_ref, wd_ref, wout_ref, postg_ref, preg_ref, x1_ref, h2_ref):
    tm = h_ref.shape[0]
    row_blocks = [slice(r0, r0 + MERGE_ROWS) for r0 in range(0, tm, MERGE_ROWS)]
    mod = mod_ref[...]
    g1 = mod[:, 2 * D_MODEL:3 * D_MODEL]
    sh2 = mod[:, 3 * D_MODEL:4 * D_MODEL]
    sc2 = mod[:, 4 * D_MODEL:5 * D_MODEL]
    merged = []
    for rs in row_blocks:
        hb = h_ref[rs, :]
        acc = None
        for k, (f_ref, w_ref) in enumerate(((fa_ref, wa_ref), (fb_ref, wb_ref), (fc_ref, wc_ref), (fd_ref, wd_ref))):
            cols = slice(k * D_MODEL, (k + 1) * D_MODEL)
            gate = _sigmoid(_dot(hb, wg_ref[:, cols]) + bg_ref[:, cols])
            t = gate * _dot(f_ref[rs, :], w_ref[...])
            acc = t if acc is None else acc + t
        merged.append(acc.astype(BF16))
    mixes = [_dot(m, wout_ref[...]) for m in merged]
    for rs, mix in zip(row_blocks, mixes):
        x1 = x_ref[rs, :] + g1 * _rms(mix, postg_ref[...])
        x1_ref[rs, :] = x1
        h2_ref[rs, :] = (_rms(x1, preg_ref[...]) * (1.0 + sc2) + sh2).astype(BF16)


def _merge_call(x2d, mod3, mod_row_of_tile, h, fa, fb, fc, fd, lw):
    rows = x2d.shape[0]
    tm = ROW_TILE
    names = ["w_gate", "b_gate", "w_o_mla", "w_conv_out", "w_gmlp_out", "w_pool_out", "w_out",
             "post_mix_g", "pre_ffn_g"]
    row_spec = lambda w: pl.BlockSpec((tm, w), lambda i: (i, 0))
    return pl.pallas_call(
        _merge_kernel,
        grid=(rows // tm,),
        in_specs=[row_spec(D_MODEL),
                  pl.BlockSpec((None, 1, 6 * D_MODEL), lambda i: (mod_row_of_tile(i), 0, 0)),
                  row_spec(D_MODEL), row_spec(MLA_W), row_spec(CONV_W), row_spec(GMLP_W), row_spec(POOL_W)]
                 + [_layer_spec(lw.stacked[n], lw.layer) for n in names],
        out_specs=[row_spec(D_MODEL), row_spec(D_MODEL)],
        out_shape=[jax.ShapeDtypeStruct((rows, D_MODEL), F32), jax.ShapeDtypeStruct((rows, D_MODEL), BF16)],
        compiler_params=_params(1),
        name="merge",
    )(x2d, mod3, h, fa, fb, fc, fd, *[lw.stacked[n] for n in names])


def _ffn_kernel(seq_len, h2_ref, x1_ref, mod_ref, wup_ref, dw_ref, dwb_ref, wdn_ref, postg_ref, o_ref, act_ref):
    tm = h2_ref.shape[0]
    assert seq_len & (seq_len - 1) == 0
    pos = lax.broadcasted_iota(jnp.int32, (tm, FFN_CHUNK), 0) & (seq_len - 1)
    first = pos == 0
    last = pos == seq_len - 1

    def conv_cols(c0):
        cols = slice(c0, c0 + FFN_CHUNK)
        up = _dot(h2_ref[...], wup_ref[:, cols])
        dw = dw_ref[:, cols]
        prev = jnp.where(first, 0.0, pltpu.roll(up, 1, 0))
        nxt = jnp.where(last, 0.0, pltpu.roll(up, tm - 1, 0))
        return prev * dw[0:1, :] + up * dw[1:2, :] + nxt * dw[2:3, :] + dwb_ref[:, cols]

    for c in range(FFN_NCHUNK):
        c0 = c * FFN_CHUNK
        g = conv_cols(c0)
        val = conv_cols(D_FF + c0)
        act_ref[:, c0:c0 + FFN_CHUNK] = (g * _sigmoid(g) * val).astype(BF16)
    f = _dot(act_ref[...], wdn_ref[...])
    g2 = mod_ref[...][:, 5 * D_MODEL:6 * D_MODEL]
    o_ref[...] = x1_ref[...] + g2 * _rms(f, postg_ref[...])


def _ffn_call(h2, x1, mod3, seq_len, mod_row_of_tile, lw):
    rows = h2.shape[0]
    tm = FFN_TILE
    names = ["ffn_up", "ffn_dw", "ffn_dw_b", "ffn_down", "post_ffn_g"]
    row_spec = pl.BlockSpec((tm, D_MODEL), lambda i: (i, 0))
    return pl.pallas_call(
        functools.partial(_ffn_kernel, seq_len),
        grid=(rows // tm,),
        in_specs=[row_spec, row_spec,
                  pl.BlockSpec((None, 1, 6 * D_MODEL), lambda i: (mod_row_of_tile(i), 0, 0))]
                 + [_layer_spec(lw.stacked[n], lw.layer) for n in names],
        out_specs=row_spec,
        out_shape=jax.ShapeDtypeStruct((rows, D_MODEL), F32),
        scratch_shapes=[pltpu.VMEM((tm, D_FF), BF16)],
        compiler_params=_params(1),
        name="ffn",
    )(h2, x1, mod3, *[lw.stacked[n] for n in names])


def _rope_tables(seq_len):
    quarter = QK_ROPE // 4
    inv = ROPE_BASE ** (-jnp.arange(quarter, dtype=F32) / quarter)
    t = jnp.arange(seq_len)
    row = (t // GRID_W).astype(F32)[:, None] * inv[None, :]
    col = (t % GRID_W).astype(F32)[:, None] * inv[None, :]
    ang = jnp.concatenate([row, row, col, col], axis=1)
    cos, sin = jnp.cos(ang), jnp.sin(ang)
    ones = jnp.ones((seq_len, QK_NOPE), F32)
    zeros_n = jnp.zeros((seq_len, QK_NOPE), F32)
    zeros_t = jnp.zeros((seq_len, HEAD_SLOT - QK_NOPE - QK_ROPE), F32)
    ccat = jnp.concatenate([ones, cos, zeros_t], axis=1)
    scat = jnp.concatenate([zeros_n, sin, zeros_t], axis=1)
    zeros_k = jnp.zeros((seq_len, LANES - QK_ROPE), F32)
    ckpe = jnp.concatenate([cos, zeros_k], axis=1)
    skpe = jnp.concatenate([sin, zeros_k], axis=1)
    return ccat, scat, ckpe, skpe


class _LayerWeights:
    def __init__(self, stacked, e_kpe, layer):
        self.stacked = stacked
        self.e_kpe = e_kpe
        self.layer = layer


def _rot_half(x):
    quarter = QK_ROPE // 4
    parts = []
    for b0 in range(0, QK_ROPE, 2 * quarter):
        parts += [-x[..., b0 + quarter:b0 + 2 * quarter], x[..., b0:b0 + quarter]]
    return jnp.concatenate(parts, axis=-1)


def _split_w_in_kernel(w_ref, small_ref, gate_ref):
    w = w_ref[...]
    gate_ref[...] = w[:, OFF_GATE:].astype(BF16)
    small_ref[:, S_Q:S_CONV] = w[:, 0:OFF_KPE].astype(BF16)
    small_ref[:, S_CONV:S_KPE] = w[:, OFF_CONV:OFF_GATE].astype(BF16)
    kpe = w[:, OFF_KPE:OFF_CONV]
    zeros = jnp.zeros((w.shape[0], LANES - QK_ROPE), F32)
    small_ref[:, S_KPE:S_KPEP] = jnp.concatenate([kpe, zeros], axis=1).astype(BF16)
    small_ref[:, S_KPEP:S_TOTAL] = jnp.concatenate([_rot_half(kpe), zeros], axis=1).astype(BF16)


def _split_w_in_call(w_in):
    depth, rows, cols = w_in.shape
    tr = 256
    return pl.pallas_call(
        _split_w_in_kernel,
        grid=(depth, rows // tr),
        in_specs=[pl.BlockSpec((None, tr, cols), lambda l, i: (l, i, 0))],
        out_specs=[pl.BlockSpec((None, tr, S_TOTAL), lambda l, i: (l, i, 0)),
                   pl.BlockSpec((None, tr, N_BRANCH * D_MODEL), lambda l, i: (l, i, 0))],
        out_shape=[jax.ShapeDtypeStruct((depth, rows, S_TOTAL), BF16),
                   jax.ShapeDtypeStruct((depth, rows, N_BRANCH * D_MODEL), BF16)],
        compiler_params=_params(2),
        name="split_w_in",
    )(w_in)


def _prep_weights(W):
    depth = W["w_in"].shape[0]
    w_small, w_gate = _split_w_in_call(W["w_in"])

    wq = W["w_q_b"].reshape(depth, Q_LORA, MLA_HEADS, QK_NOPE + QK_ROPE)
    q_nope, q_pe = wq[..., :QK_NOPE], wq[..., QK_NOPE:]
    tail = jnp.zeros((depth, Q_LORA, MLA_HEADS, HEAD_SLOT - QK_NOPE - QK_ROPE), F32)
    wq_cat = jnp.concatenate([q_nope, q_pe, tail], axis=3).reshape(depth, Q_LORA, QK_CAT)
    wq_rot = jnp.concatenate([jnp.zeros_like(q_nope), _rot_half(q_pe), tail], axis=3)
    wq_rope = jnp.concatenate([wq_cat, wq_rot.reshape(depth, Q_LORA, QK_CAT)], axis=2)

    wkv = W["w_kv_b"].reshape(depth, KV_LORA, MLA_HEADS, QK_NOPE + V_HEAD)
    k_nope, v_w = wkv[..., :QK_NOPE], wkv[..., QK_NOPE:]
    k_cat = jnp.concatenate([k_nope, jnp.zeros((depth, KV_LORA, MLA_HEADS, HEAD_SLOT - QK_NOPE), F32)], axis=3)
    wkv_cat = jnp.concatenate([k_cat.reshape(depth, KV_LORA, QK_CAT), v_w.reshape(depth, KV_LORA, MLA_W)], axis=2)

    e = np.zeros((LANES, QK_CAT), np.float32)
    for hd in range(MLA_HEADS):
        for d in range(QK_ROPE):
            e[d, hd * HEAD_SLOT + QK_NOPE + d] = 1.0

    eye = jnp.eye(POOL_GROUPS, dtype=F32)
    pool_bd = (eye[None, :, None, :, None] * W["pool_w"][:, :, :, None, :]).reshape(depth, POOL_W, POOL_W)

    row = lambda a: a.reshape(depth, 1, -1)
    stacked = {
        "pre_mix_g": row(W["pre_mix_g"]), "post_mix_g": row(W["post_mix_g"]),
        "pre_ffn_g": row(W["pre_ffn_g"]), "post_ffn_g": row(W["post_ffn_g"]),
        "w_small": w_small, "w_gate": w_gate, "b_gate": row(W["b_gate"]),
        "q_norm_g": row(W["q_norm_g"]), "wq": wq_cat.astype(BF16), "wq_rope": wq_rope.astype(BF16),
        "kv_norm_g": row(W["kv_norm_g"]), "wkv": wkv_cat.astype(BF16),
        "w_o_mla": W["w_o_mla"].astype(BF16),
        "conv_dw": W["conv_dw"], "conv_dw_b": row(W["conv_dw_b"]),
        "conv_ln_g": row(W["conv_ln_g"]), "conv_ln_b": row(W["conv_ln_b"]),
        "w_conv_out": W["w_conv_out"].astype(BF16),
        "gmlp_ln_g": row(W["gmlp_ln_g"]), "gmlp_ln_b": row(W["gmlp_ln_b"]),
        "gmlp_wcat": W["gmlp_ws"].transpose(0, 2, 1, 3).reshape(depth, GMLP_CHUNK, GMLP_GROUPS * GMLP_CHUNK).astype(BF16),
        "gmlp_bmat": jnp.repeat(W["gmlp_bs"].transpose(0, 2, 1), GMLP_W // GMLP_GROUPS, axis=2),
        "w_gmlp_out": W["w_gmlp_out"].astype(BF16),
        "pool_bd": pool_bd.astype(BF16),
        "pool_scale": row(W["pool_scale"]),
        "w_pool_out": W["w_pool_out"].astype(BF16),
        "w_out": W["w_out"].astype(BF16),
        "ffn_up": W["ffn_up"].astype(BF16),
        "ffn_dw": W["ffn_dw"],
        "ffn_dw_b": row(W["ffn_dw_b"]),
        "ffn_down": W["ffn_down"].astype(BF16),
    }
    return stacked, jnp.asarray(e, BF16)


def _trunk_layer(x2d, seq_len, mod3, mod_row_of_tile_small, mod_row_of_tile_ffn, lw, rope_tabs, ctx):
    p = _rowproj_call(x2d, mod3, seq_len, mod_row_of_tile_small, lw, rope_tabs)
    fb, fd = _seqmix_call(p["a"], p["pin"], seq_len, lw)
    if ctx is None:
        fa = _attn_call([p["q"]], [p["k"]], [p["v"]], seq_len)
    else:
        k_ctx, v_ctx = ctx
        fa = _attn_call([p["q"], p["qp"]], [p["k"], k_ctx], [p["v"], v_ctx], seq_len)
    x1, h2 = _merge_call(x2d, mod3, mod_row_of_tile_small, p["h"], fa, fb, p["fc"], fd, lw)
    x2 = _ffn_call(h2, x1, mod3, seq_len, mod_row_of_tile_ffn, lw)
    return x2, p["ckv"], p["kpe"]


def kernel(x_prompt, x_sample, cache_ckv, cache_kpe, c, c_ctx, ada_w, ada_b, pre_mix_g, post_mix_g, pre_ffn_g, post_ffn_g, w_in, b_gate, q_norm_g, w_q_b, kv_norm_g, w_kv_b, w_o_mla, conv_dw, conv_dw_b, conv_ln_g, conv_ln_b, w_conv_out, gmlp_ln_g, gmlp_ln_b, gmlp_ws, gmlp_bs, w_gmlp_out, pool_w, pool_scale, w_pool_out, w_out, ffn_up, ffn_dw, ffn_dw_b, ffn_down):
    W = dict(pre_mix_g=pre_mix_g, post_mix_g=post_mix_g, pre_ffn_g=pre_ffn_g, post_ffn_g=post_ffn_g,
             w_in=w_in, b_gate=b_gate, q_norm_g=q_norm_g, w_q_b=w_q_b, kv_norm_g=kv_norm_g,
             w_kv_b=w_kv_b, w_o_mla=w_o_mla, conv_dw=conv_dw, conv_dw_b=conv_dw_b,
             conv_ln_g=conv_ln_g, conv_ln_b=conv_ln_b, w_conv_out=w_conv_out, gmlp_ln_g=gmlp_ln_g,
             gmlp_ln_b=gmlp_ln_b, gmlp_ws=gmlp_ws, gmlp_bs=gmlp_bs, w_gmlp_out=w_gmlp_out,
             pool_w=pool_w, pool_scale=pool_scale, w_pool_out=w_pool_out, w_out=w_out,
             ffn_up=ffn_up, ffn_dw=ffn_dw, ffn_dw_b=ffn_dw_b, ffn_down=ffn_down)
    n_prompt, prompt_len, _ = x_prompt.shape
    n_sample, sample_len, _ = x_sample.shape
    past_len = cache_ckv.shape[2]
    ctx_row = n_sample

    cvec = jnp.zeros((SUBLANES, D_MODEL), F32).at[0:n_sample].set(c).at[ctx_row].set(c_ctx)
    mod = _mod_call(cvec, ada_w, ada_b)

    rope_tabs = _rope_tables(sample_len)
    xp = x_prompt.reshape(n_prompt * prompt_len, D_MODEL)
    xs = x_sample.reshape(n_sample * sample_len, D_MODEL)
    ckv_list, kpe_list = [], []
    stacked, e_kpe = _prep_weights(W)
    for l in range(DEPTH):
        mod3 = mod[l].reshape(SUBLANES, 1, 6 * D_MODEL)
        lw = _LayerWeights(stacked, e_kpe, l)
        xp, ckv_l, kpe_l = _trunk_layer(xp, prompt_len, mod3, lambda i: ctx_row, lambda i: ctx_row,
                                        lw, None, None)
        ckv_list.append(ckv_l.reshape(n_prompt, prompt_len, KV_LORA))
        kpe_list.append(kpe_l.reshape(n_prompt, prompt_len, QK_ROPE))
        kpe_pad = jnp.pad(cache_kpe[:, l].reshape(n_sample * past_len, QK_ROPE),
                          ((0, 0), (0, LANES - QK_ROPE)))
        ctx = _ctxkv_call(cache_ckv[:, l].reshape(n_sample * past_len, KV_LORA), kpe_pad, lw)
        xs, _, _ = _trunk_layer(xs, sample_len, mod3,
                                lambda i: (i * ROW_TILE) // sample_len,
                                lambda i: (i * FFN_TILE) // sample_len,
                                lw, rope_tabs, ctx)
    new_ckv = jnp.stack(ckv_list, axis=1)
    new_kpe = jnp.stack(kpe_list, axis=1)
    return (xp.reshape(n_prompt, prompt_len, D_MODEL), xs.reshape(n_sample, sample_len, D_MODEL),
            new_ckv, new_kpe)
```

```python
import functools

import jax
import jax.numpy as jnp
import numpy as np
from jax import lax
from jax.experimental import pallas as pl
from jax.experimental.pallas import tpu as pltpu

D_MODEL = 1024
DEPTH = 2
GRID_W = 64
MLA_HEADS = 8
Q_LORA = 384
KV_LORA = 256
QK_NOPE = 64
QK_ROPE = 32
V_HEAD = 64
MLA_W = MLA_HEADS * V_HEAD
SOFTMAX_SCALE = (QK_NOPE + QK_ROPE) ** -0.5
ROPE_BASE = 10000.0
CONV_W = 256
CONV_K = 31
GMLP_W = 256
GMLP_GROUPS = 4
GMLP_CHUNK = 128
POOL_W = 256
POOL_GROUPS = 4
N_BRANCH = 4
D_FF = 2816
FFN_K = 3
NORM_EPS = 1e-6
OFF_KV = Q_LORA
OFF_KPE = OFF_KV + KV_LORA
OFF_CONV = OFF_KPE + QK_ROPE
OFF_GMLP = OFF_CONV + 2 * CONV_W
OFF_POOL = OFF_GMLP + 2 * GMLP_W
OFF_GATE = OFF_POOL + POOL_W

LANES = 128
SUBLANES = 8
VMEM_LIMIT_BYTES = 56 * 1024 * 1024

S_Q = 0
S_KV = S_Q + Q_LORA
S_CONV = S_KV + KV_LORA
S_GMLP = S_CONV + 2 * CONV_W
S_POOL = S_GMLP + 2 * GMLP_W
S_KPE = S_POOL + POOL_W
S_KPEP = S_KPE + LANES
S_TOTAL = S_KPEP + LANES

HEAD_SLOT = LANES
QK_CAT = MLA_HEADS * HEAD_SLOT

ROW_TILE = 512
MERGE_ROWS = 256
FFN_TILE = 1024
FFN_CHUNK = 256
FFN_NCHUNK = D_FF // FFN_CHUNK
SEQ_ROWS = 64
PAD_ROWS = 16
ATTN_Q_TILE = 256
ATTN_GROUP = 4

BF16 = jnp.bfloat16
F32 = jnp.float32
INV_SQRT2 = 0.7071067811865476
LOG2E = 1.4426950408889634


def _dot(a, b):
    return jnp.dot(a, b, preferred_element_type=F32)


def _dot_nt(a, b):
    return lax.dot_general(a, b, (((1,), (1,)), ((), ())), preferred_element_type=F32)


def _rms(x, g):
    return x * lax.rsqrt(jnp.mean(x * x, axis=-1, keepdims=True) + NORM_EPS) * g


def _layernorm(x, g, b):
    mu = jnp.mean(x, axis=-1, keepdims=True)
    xc = x - mu
    var = jnp.mean(xc * xc, axis=-1, keepdims=True)
    return xc * lax.rsqrt(var + NORM_EPS) * g + b


def _sigmoid(x):
    return 1.0 / (1.0 + jnp.exp(-x))


def _const_spec(shape):
    nd = len(shape)
    return pl.BlockSpec(shape, lambda *_: (0,) * nd, pipeline_mode=pl.Buffered(1))


def _layer_spec(stacked, layer):
    tail = stacked.shape[1:]
    return pl.BlockSpec((None,) + tail, lambda *_: (layer,) + (0,) * len(tail), pipeline_mode=pl.Buffered(1))


def _params(n_axes, flags=None):
    return pltpu.CompilerParams(dimension_semantics=("arbitrary",) * n_axes,
                                vmem_limit_bytes=VMEM_LIMIT_BYTES, flags=flags)


def _mod_kernel(cv_ref, w_ref, b_ref, o_ref):
    cv = cv_ref[...]
    s = (cv * _sigmoid(cv)).astype(BF16)
    o_ref[0] = _dot(s, w_ref[0].astype(BF16)) + b_ref[0]


def _mod_call(cvec, ada_w, ada_b):
    tn = 1536
    n = 6 * D_MODEL
    return pl.pallas_call(
        _mod_kernel,
        grid=(DEPTH, n // tn),
        in_specs=[pl.BlockSpec((SUBLANES, D_MODEL), lambda l, j: (0, 0)),
                  pl.BlockSpec((1, D_MODEL, tn), lambda l, j: (l, 0, j)),
                  pl.BlockSpec((1, 1, tn), lambda l, j: (l, 0, j))],
        out_specs=pl.BlockSpec((1, SUBLANES, tn), lambda l, j: (l, 0, j)),
        out_shape=jax.ShapeDtypeStruct((DEPTH, SUBLANES, n), F32),
        compiler_params=_params(2),
        name="mod",
    )(cvec, ada_w, ada_b.reshape(DEPTH, 1, n))


def _rowproj_kernel(rope, seq_len, x_ref, mod_ref, e_ref, preg_ref, ws_ref, qng_ref, wq_ref, kvng_ref, wkv_ref,
                    glng_ref, glnb_ref, gws_ref, gbm_ref, cw_ref, cb_ref, clng_ref, clnb_ref, pw_ref, ps_ref, *rest):
    rest = list(rest)
    if rope:
        ccat_ref, scat_ref, ckpe_ref, skpe_ref = rest[:4]
        rest = rest[4:]
    apad_ref, ppad_ref, pooled_ref = rest[-3:]
    rest = rest[:-3]
    h_ref, q_ref = rest[:2]
    rest = rest[2:]
    if rope:
        qp_ref = rest.pop(0)
    k_ref, v_ref = rest[:2]
    rest = rest[2:]
    if not rope:
        ckv_ref, kpe_ref = rest[:2]
        rest = rest[2:]
    fb_ref, fd_ref, fc_ref = rest
    x = x_ref[...]
    tm = x.shape[0]
    mod = mod_ref[...]
    sh1 = mod[:, 0:D_MODEL]
    sc1 = mod[:, D_MODEL:2 * D_MODEL]
    h = _rms(x, preg_ref[...]) * (1.0 + sc1) + sh1
    hb = h.astype(BF16)
    h_ref[...] = hb

    ci = _dot(hb, ws_ref[:, S_CONV:S_GMLP])
    pin = _dot(hb, ws_ref[:, S_POOL:S_KPE])
    a = ci[:, 0:CONV_W] * _sigmoid(ci[:, CONV_W:2 * CONV_W])
    _seq_branches(seq_len, tm // seq_len, a, pin, cw_ref, cb_ref, clng_ref, clnb_ref, fb_ref,
                  apad_ref, ppad_ref, pooled_ref)
    qa = _dot(hb, ws_ref[:, S_Q:S_KV])
    kvc = _dot(hb, ws_ref[:, S_KV:S_CONV])
    gm = _dot(hb, ws_ref[:, S_GMLP:S_POOL])
    kpa = _dot(hb, ws_ref[:, S_KPE:S_KPEP])
    if rope:
        kpb = _dot(hb, ws_ref[:, S_KPEP:S_TOTAL])

    qn = _rms(qa, qng_ref[...]).astype(BF16)
    ckv = _rms(kvc, kvng_ref[...])
    if not rope:
        ckv_ref[...] = ckv
        kpe_ref[...] = kpa[:, 0:QK_ROPE]
    kpr = kpa * ckpe_ref[...] + kpb * skpe_ref[...] if rope else kpa
    qq = _dot(qn, wq_ref[...]) * (SOFTMAX_SCALE * LOG2E)
    kvb = _dot(ckv.astype(BF16), wkv_ref[...])
    kpx = _dot(kpr.astype(BF16), e_ref[...])
    if rope:
        ccat = ccat_ref[...]
        scat = scat_ref[...]
        for hd in range(MLA_HEADS):
            lo = hd * HEAD_SLOT
            qpl = qq[:, lo:lo + HEAD_SLOT]
            qpr = qq[:, QK_CAT + lo:QK_CAT + lo + HEAD_SLOT]
            q_ref[:, lo:lo + HEAD_SLOT] = (qpl * ccat + qpr * scat).astype(BF16)
            qp_ref[:, lo:lo + HEAD_SLOT] = qpl.astype(BF16)
    else:
        q_ref[...] = qq.astype(BF16)
    k_ref[...] = (kvb[:, 0:QK_CAT] + kpx).astype(BF16)
    v_ref[...] = kvb[:, QK_CAT:QK_CAT + MLA_W].astype(BF16)

    uv = 0.5 * gm * (1.0 + lax.erf(gm * INV_SQRT2))
    u = uv[:, 0:GMLP_W]
    v = _layernorm(uv[:, GMLP_W:2 * GMLP_W], glng_ref[...], glnb_ref[...])
    grp = lax.broadcasted_iota(jnp.int32, (GMLP_CHUNK, GMLP_W), 1) // (GMLP_W // GMLP_GROUPS)
    gws = gws_ref[...]
    gbm = gbm_ref[...]
    for n in range(tm // GMLP_CHUNK):
        r0 = n * GMLP_CHUNK
        vc = v[r0:r0 + GMLP_CHUNK]
        vs = jnp.concatenate([jnp.where(grp == g, vc, 0.0) for g in range(GMLP_GROUPS)], axis=0)
        sv = _dot(gws, vs.astype(BF16)) + gbm
        fc_ref[r0:r0 + GMLP_CHUNK, :] = (u[r0:r0 + GMLP_CHUNK] * sv).astype(BF16)
    fd_ref[...] = (_dot(pooled_ref[...].astype(BF16), pw_ref[...]) * ps_ref[...]).astype(BF16)


def _rowproj_call(x2d, mod3, seq_len, mod_row_of_row, lw, rope_tabs):
    rows = x2d.shape[0]
    tm = max(ROW_TILE, seq_len)
    assert tm % seq_len == 0 and rows % tm == 0
    n_seq = tm // seq_len
    rope = rope_tabs is not None
    names = ["pre_mix_g", "w_small", "q_norm_g", "wq_rope" if rope else "wq", "kv_norm_g", "wkv",
             "gmlp_ln_g", "gmlp_ln_b", "gmlp_wcat", "gmlp_bmat",
             "conv_dw", "conv_dw_b", "conv_ln_g", "conv_ln_b", "pool_bd", "pool_scale"]
    row_spec = lambda w: pl.BlockSpec((tm, w), lambda i: (i, 0))
    in_specs = [row_spec(D_MODEL),
                pl.BlockSpec((None, 1, 6 * D_MODEL), lambda i: (mod_row_of_row(i * tm), 0, 0)),
                _const_spec((LANES, QK_CAT))] + [_layer_spec(lw.stacked[n], lw.layer) for n in names]
    args = [x2d, mod3, lw.e_kpe] + [lw.stacked[n] for n in names]
    outs = [("h", D_MODEL, BF16), ("q", QK_CAT, BF16)]
    if rope:
        tiles_per_seq = seq_len // tm
        in_specs += [pl.BlockSpec((tm, LANES), lambda i: (i % tiles_per_seq, 0))] * 4
        args += list(rope_tabs)
        outs.append(("qp", QK_CAT, BF16))
    outs += [("k", QK_CAT, BF16), ("v", MLA_W, BF16)]
    if not rope:
        outs += [("ckv", KV_LORA, F32), ("kpe", QK_ROPE, F32)]
    outs += [("fb", CONV_W, BF16), ("fd", POOL_W, BF16), ("fc", GMLP_W, BF16)]
    span = seq_len + 2 * PAD_ROWS
    res = pl.pallas_call(
        functools.partial(_rowproj_kernel, rope, seq_len),
        grid=(rows // tm,),
        in_specs=in_specs,
        out_specs=[row_spec(w) for _, w, _ in outs],
        out_shape=[jax.ShapeDtypeStruct((rows, w), dt) for _, w, dt in outs],
        scratch_shapes=[pltpu.VMEM((n_seq * span, CONV_W), F32), pltpu.VMEM((n_seq * span, POOL_W), F32),
                        pltpu.VMEM((tm, POOL_W), F32)],
        compiler_params=_params(1),
        name="rowproj_rope" if rope else "rowproj",
    )(*args)
    return {name: r for (name, _, _), r in zip(outs, res)}


def _seq_branches(seq_len, n_seq, a, pin, cw_ref, cb_ref, clng_ref, clnb_ref, fb_ref,
                  apad_ref, ppad_ref, pooled_ref):
    L = seq_len
    span = L + 2 * PAD_ROWS
    zeros = jnp.zeros((PAD_ROWS, CONV_W), F32)
    for s in range(n_seq):
        for pad_ref, src in ((apad_ref, a), (ppad_ref, pin)):
            pad_ref[s * span:s * span + PAD_ROWS, :] = zeros
            pad_ref[s * span + L + PAD_ROWS:(s + 1) * span, :] = zeros
            pad_ref[s * span + PAD_ROWS:s * span + PAD_ROWS + L, :] = src[s * L:(s + 1) * L]
    cb = cb_ref[...]
    clng = clng_ref[...]
    clnb = clnb_ref[...]
    half_conv = CONV_K // 2
    lane = lax.broadcasted_iota(jnp.int32, (SEQ_ROWS, POOL_W), 1)
    grp_w = POOL_W // POOL_GROUPS
    half = jnp.where(lane < grp_w, 1, jnp.where(lane < 2 * grp_w, 2, jnp.where(lane < 3 * grp_w, 4, 8)))
    win = SEQ_ROWS + 2 * PAD_ROWS

    def shifted_windows(pad_ref, p0):
        w = pad_ref[p0:p0 + win, :]
        return [w] + [pltpu.roll(w, win - b, 0) for b in range(1, SUBLANES)]

    def tap_of(xs, off):
        a0 = off // SUBLANES * SUBLANES
        return xs[off % SUBLANES][a0:a0 + SEQ_ROWS]

    for s in range(n_seq):
        for r0 in range(0, L, SEQ_ROWS):
            out_rows = slice(s * L + r0, s * L + r0 + SEQ_ROWS)
            xs = shifted_windows(apad_ref, s * span + r0)
            acc = jnp.zeros((SEQ_ROWS, CONV_W), F32) + cb
            for k in range(CONV_K):
                acc = acc + tap_of(xs, PAD_ROWS + k - half_conv) * cw_ref[pl.ds(k, 1), :]
            y = _layernorm(acc, clng, clnb)
            fb_ref[out_rows, :] = (y * _sigmoid(y)).astype(BF16)

            ps = shifted_windows(ppad_ref, s * span + r0)
            tap = lambda j: tap_of(ps, PAD_ROWS + j)
            x0 = tap(0)
            s2 = tap(-1) + x0
            s4 = s2 + tap(-2) + tap(1)
            s8 = s4 + tap(-4) + tap(-3) + tap(2) + tap(3)
            s16 = s8 + tap(-8) + tap(-7) + tap(-6) + tap(-5) + tap(4) + tap(5) + tap(6) + tap(7)
            ssum = jnp.where(lane < grp_w, s2, jnp.where(lane < 2 * grp_w, s4, jnp.where(lane < 3 * grp_w, s8, s16)))
            t = r0 + lax.broadcasted_iota(jnp.int32, (SEQ_ROWS, POOL_W), 0)
            cnt = jnp.minimum(t + half, L) - jnp.maximum(t - half, 0)
            pooled_ref[out_rows, :] = ssum / cnt.astype(F32) - x0


def _ctxkv_kernel(ckv_ref, kpe_ref, wkv_ref, e_ref, k_ref, v_ref):
    kvb = _dot(ckv_ref[...].astype(BF16), wkv_ref[...])
    k_ref[...] = (kvb[:, 0:QK_CAT] + _dot(kpe_ref[...].astype(BF16), e_ref[...])).astype(BF16)
    v_ref[...] = kvb[:, QK_CAT:QK_CAT + MLA_W].astype(BF16)


def _ctxkv_call(ckv2d, kpe2d, lw):
    rows = ckv2d.shape[0]
    return pl.pallas_call(
        _ctxkv_kernel,
        grid=(1,),
        in_specs=[pl.BlockSpec((rows, KV_LORA), lambda i: (0, 0)),
                  pl.BlockSpec((rows, LANES), lambda i: (0, 0)),
                  _layer_spec(lw.stacked["wkv"], lw.layer), _const_spec((LANES, QK_CAT))],
        out_specs=[pl.BlockSpec((rows, QK_CAT), lambda i: (0, 0)), pl.BlockSpec((rows, MLA_W), lambda i: (0, 0))],
        out_shape=[jax.ShapeDtypeStruct((rows, QK_CAT), BF16), jax.ShapeDtypeStruct((rows, MLA_W), BF16)],
        compiler_params=_params(1),
        name="ctxkv",
    )(ckv2d, kpe2d, lw.stacked["wkv"], lw.e_kpe)


def _attn_kernel(nseg, group, q_rows, k_rows, *refs):
    q_refs = refs[0:nseg]
    k_refs = refs[nseg:2 * nseg]
    v_refs = refs[2 * nseg:3 * nseg]
    o_ref = refs[3 * nseg]
    out_lane = lax.broadcasted_iota(jnp.int32, (q_rows, LANES), 1)
    for g in range(group):
        qr = slice(g * q_rows, (g + 1) * q_rows)
        krs = [slice(g * n, (g + 1) * n) for n in k_rows]
        scores = [[_dot_nt(q_refs[s][qr, hd * HEAD_SLOT:(hd + 1) * HEAD_SLOT],
                           k_refs[s][krs[s], hd * HEAD_SLOT:(hd + 1) * HEAD_SLOT]) for s in range(nseg)]
                  for hd in range(MLA_HEADS)]
        probs = []
        for hd in range(MLA_HEADS):
            m = functools.reduce(jnp.maximum, [jnp.max(sc, axis=-1, keepdims=True) for sc in scores[hd]])
            probs.append([jnp.exp2(sc - m).astype(BF16) for sc in scores[hd]])
        for pair in range(MLA_HEADS // 2):
            plo = pair * LANES
            outs = []
            for hh in range(2):
                o = None
                for s in range(nseg):
                    vp = v_refs[s][krs[s], plo:plo + LANES]
                    lane = lax.broadcasted_iota(jnp.int32, vp.shape, 1)
                    own = (lane < V_HEAD) if hh == 0 else (lane >= V_HEAD)
                    t = _dot(probs[2 * pair + hh][s], jnp.where(own, vp, jnp.ones_like(vp)))
                    o = t if o is None else o + t
                outs.append(o * (1.0 / pltpu.roll(o, V_HEAD, 1)))
            o_ref[qr, plo:plo + LANES] = jnp.where(out_lane < V_HEAD, outs[0], outs[1]).astype(BF16)


def _attn_call(qs, ks, vs, seq_len):
    nseg = len(qs)
    rows = qs[0].shape[0]
    nb = rows // seq_len
    tq = min(ATTN_Q_TILE, seq_len)
    nq = seq_len // tq
    group = ATTN_GROUP if nq == 1 else 1
    k_rows = [k.shape[0] // nb for k in ks]
    q_spec = pl.BlockSpec((group * tq, QK_CAT), lambda b, j: (b * nq + j, 0))
    k_specs = [pl.BlockSpec((group * n, QK_CAT), lambda b, j: (b, 0)) for n in k_rows]
    v_specs = [pl.BlockSpec((group * n, MLA_W), lambda b, j: (b, 0)) for n in k_rows]
    return pl.pallas_call(
        functools.partial(_attn_kernel, nseg, group, tq, k_rows),
        grid=(nb // group, nq),
        in_specs=[q_spec] * nseg + k_specs + v_specs,
        out_specs=pl.BlockSpec((group * tq, MLA_W), lambda b, j: (b * nq + j, 0)),
        out_shape=jax.ShapeDtypeStruct((rows, MLA_W), BF16),
        compiler_params=_params(2),
        name="attn%d" % nseg,
    )(*qs, *ks, *vs)


def _merge_kernel(x_ref, mod_ref, h_ref, fa_ref, fb_ref, fc_ref, fd_ref, wg_ref, bg_ref,
                  wa_ref, wb_ref, wc_ref, wd_ref, wout_ref, postg_ref, preg_ref, x1_ref, h2_ref):
    tm = h_ref.shape[0]
    row_blocks = [slice(r0, r0 + MERGE_ROWS) for r0 in range(0, tm, MERGE_ROWS)]
    mod = mod_ref[...]
    g1 = mod[:, 2 * D_MODEL:3 * D_MODEL]
    sh2 = mod[:, 3 * D_MODEL:4 * D_MODEL]
    sc2 = mod[:, 4 * D_MODEL:5 * D_MODEL]
    merged = []
    for rs in row_blocks:
        hb = h_ref[rs, :]
        acc = None
        for k, (f_ref, w_ref) in enumerate(((fa_ref, wa_ref), (fb_ref, wb_ref), (fc_ref, wc_ref), (fd_ref, wd_ref))):
            cols = slice(k * D_MODEL, (k + 1) * D_MODEL)
            gate = _sigmoid(_dot(hb, wg_ref[:, cols]) + bg_ref[:, cols])
            t = gate * _dot(f_ref[rs, :], w_ref[...])
            acc = t if acc is None else acc + t
        merged.append(acc.astype(BF16))
    mixes = [_dot(m, wout_ref[...]) for m in merged]
    for rs, mix in zip(row_blocks, mixes):
        x1 = x_ref[rs, :] + g1 * _rms(mix, postg_ref[...])
        x1_ref[rs, :] = x1
        h2_ref[rs, :] = (_rms(x1, preg_ref[...]) * (1.0 + sc2) + sh2).astype(BF16)


def _merge_call(x2d, mod3, mod_row_of_row, h, fa, fb, fc, fd, lw):
    rows = x2d.shape[0]
    tm = ROW_TILE
    names = ["w_gate", "b_gate", "w_o_mla", "w_conv_out", "w_gmlp_out", "w_pool_out", "w_out",
             "post_mix_g", "pre_ffn_g"]
    row_spec = lambda w: pl.BlockSpec((tm, w), lambda i: (i, 0))
    return pl.pallas_call(
        _merge_kernel,
        grid=(rows // tm,),
        in_specs=[row_spec(D_MODEL),
                  pl.BlockSpec((None, 1, 6 * D_MODEL), lambda i: (mod_row_of_row(i * tm), 0, 0)),
                  row_spec(D_MODEL), row_spec(MLA_W), row_spec(CONV_W), row_spec(GMLP_W), row_spec(POOL_W)]
                 + [_layer_spec(lw.stacked[n], lw.layer) for n in names],
        out_specs=[row_spec(D_MODEL), row_spec(D_MODEL)],
        out_shape=[jax.ShapeDtypeStruct((rows, D_MODEL), F32), jax.ShapeDtypeStruct((rows, D_MODEL), BF16)],
        compiler_params=_params(1),
        name="merge",
    )(x2d, mod3, h, fa, fb, fc, fd, *[lw.stacked[n] for n in names])


def _ffn_kernel(seq_len, h2_ref, x1_ref, mod_ref, wup_ref, dw_ref, dwb_ref, wdn_ref, postg_ref, o_ref, act_ref):
    tm = h2_ref.shape[0]
    assert seq_len & (seq_len - 1) == 0
    pos = lax.broadcasted_iota(jnp.int32, (tm, FFN_CHUNK), 0) & (seq_len - 1)
    first = pos == 0
    last = pos == seq_len - 1

    def conv_cols(c0):
        cols = slice(c0, c0 + FFN_CHUNK)
        up = _dot(h2_ref[...], wup_ref[:, cols])
        dw = dw_ref[:, cols]
        prev = jnp.where(first, 0.0, pltpu.roll(up, 1, 0))
        nxt = jnp.where(last, 0.0, pltpu.roll(up, tm - 1, 0))
        return prev * dw[0:1, :] + up * dw[1:2, :] + nxt * dw[2:3, :] + dwb_ref[:, cols]

    for c in range(FFN_NCHUNK):
        c0 = c * FFN_CHUNK
        g = conv_cols(c0)
        val = conv_cols(D_FF + c0)
        act_ref[:, c0:c0 + FFN_CHUNK] = (g * _sigmoid(g) * val).astype(BF16)
    f = _dot(act_ref[...], wdn_ref[...])
    g2 = mod_ref[...][:, 5 * D_MODEL:6 * D_MODEL]
    o_ref[...] = x1_ref[...] + g2 * _rms(f, postg_ref[...])


def _ffn_call(h2, x1, mod3, seq_len, mod_row_of_row, lw):
    rows = h2.shape[0]
    tm = FFN_TILE
    names = ["ffn_up", "ffn_dw", "ffn_dw_b", "ffn_down", "post_ffn_g"]
    row_spec = pl.BlockSpec((tm, D_MODEL), lambda i: (i, 0))
    return pl.pallas_call(
        functools.partial(_ffn_kernel, seq_len),
        grid=(rows // tm,),
        in_specs=[row_spec, row_spec,
                  pl.BlockSpec((None, 1, 6 * D_MODEL), lambda i: (mod_row_of_row(i * tm), 0, 0))]
                 + [_layer_spec(lw.stacked[n], lw.layer) for n in names],
        out_specs=row_spec,
        out_shape=jax.ShapeDtypeStruct((rows, D_MODEL), F32),
        scratch_shapes=[pltpu.VMEM((tm, D_FF), BF16)],
        compiler_params=_params(1),
        name="ffn",
    )(h2, x1, mod3, *[lw.stacked[n] for n in names])


def _rope_tables(seq_len):
    quarter = QK_ROPE // 4
    inv = ROPE_BASE ** (-jnp.arange(quarter, dtype=F32) / quarter)
    t = jnp.arange(seq_len)
    row = (t // GRID_W).astype(F32)[:, None] * inv[None, :]
    col = (t % GRID_W).astype(F32)[:, None] * inv[None, :]
    ang = jnp.concatenate([row, row, col, col], axis=1)
    cos, sin = jnp.cos(ang), jnp.sin(ang)
    ones = jnp.ones((seq_len, QK_NOPE), F32)
    zeros_n = jnp.zeros((seq_len, QK_NOPE), F32)
    zeros_t = jnp.zeros((seq_len, HEAD_SLOT - QK_NOPE - QK_ROPE), F32)
    ccat = jnp.concatenate([ones, cos, zeros_t], axis=1)
    scat = jnp.concatenate([zeros_n, sin, zeros_t], axis=1)
    zeros_k = jnp.zeros((seq_len, LANES - QK_ROPE), F32)
    ckpe = jnp.concatenate([cos, zeros_k], axis=1)
    skpe = jnp.concatenate([sin, zeros_k], axis=1)
    return ccat, scat, ckpe, skpe


class _LayerWeights:
    def __init__(self, stacked, e_kpe, layer):
        self.stacked = stacked
        self.e_kpe = e_kpe
        self.layer = layer


PREP_COLS = 256
N_GATE_STEPS = N_BRANCH * D_MODEL // PREP_COLS


def _split_w_in_kernel(wt_ref, small_ref, gate_ref):
    j = pl.program_id(1)

    @pl.when(j < N_GATE_STEPS)
    def _():
        r0 = pl.multiple_of(OFF_GATE + j * PREP_COLS, SUBLANES)
        gate_ref[...] = wt_ref[pl.ds(r0, PREP_COLS), :].T.astype(BF16)

    @pl.when(j == N_GATE_STEPS)
    def _():
        def put(c0, rows_t):
            small_ref[:, c0:c0 + rows_t.shape[0]] = rows_t.T.astype(BF16)

        for r0 in range(0, OFF_KPE, LANES):
            put(S_Q + r0, wt_ref[r0:r0 + LANES, :])
        for r0 in range(OFF_CONV, OFF_GATE, LANES):
            put(S_CONV + r0 - OFF_CONV, wt_ref[r0:r0 + LANES, :])
        kpe = wt_ref[OFF_KPE:OFF_CONV, :]
        quarter = QK_ROPE // 4
        rot = []
        for b0 in range(0, QK_ROPE, 2 * quarter):
            rot += [-kpe[b0 + quarter:b0 + 2 * quarter], kpe[b0:b0 + quarter]]
        zeros = jnp.zeros((LANES - QK_ROPE, D_MODEL), F32)
        put(S_KPE, jnp.concatenate([kpe, zeros], axis=0))
        put(S_KPEP, jnp.concatenate(rot + [zeros], axis=0))


def _split_w_in_call(w_in):
    depth, d_model, cols = w_in.shape
    wt = jnp.swapaxes(w_in, 1, 2)
    last = N_GATE_STEPS - 1
    return pl.pallas_call(
        _split_w_in_kernel,
        grid=(depth, N_GATE_STEPS + 1),
        in_specs=[pl.BlockSpec((None, cols, d_model), lambda l, j: (l, 0, 0), pipeline_mode=pl.Buffered(1))],
        out_specs=[pl.BlockSpec((None, d_model, S_TOTAL), lambda l, j: (l, 0, 0)),
                   pl.BlockSpec((None, d_model, PREP_COLS), lambda l, j: (l, 0, jnp.minimum(j, last)))],
        out_shape=[jax.ShapeDtypeStruct((depth, d_model, S_TOTAL), BF16),
                   jax.ShapeDtypeStruct((depth, d_model, N_BRANCH * D_MODEL), BF16)],
        compiler_params=_params(2),
        name="split_w_in",
    )(wt)


def _prep_weights(W):
    depth = W["w_in"].shape[0]
    w_small, w_gate = _split_w_in_call(W["w_in"])

    quarter = QK_ROPE // 4
    q_head = QK_NOPE + QK_ROPE
    place_q = np.zeros((MLA_HEADS * q_head, 2 * QK_CAT), np.float32)
    place_kv = np.zeros((MLA_HEADS * (QK_NOPE + V_HEAD), QK_CAT + MLA_W), np.float32)
    e = np.zeros((LANES, QK_CAT), np.float32)
    for hd in range(MLA_HEADS):
        for j in range(q_head):
            place_q[hd * q_head + j, hd * HEAD_SLOT + j] = 1.0
        for d in range(QK_ROPE):
            first_half = d % (2 * quarter) < quarter
            src = d + quarter if first_half else d - quarter
            place_q[hd * q_head + QK_NOPE + src, QK_CAT + hd * HEAD_SLOT + QK_NOPE + d] = -1.0 if first_half else 1.0
            e[d, hd * HEAD_SLOT + QK_NOPE + d] = 1.0
        for j in range(QK_NOPE):
            place_kv[hd * (QK_NOPE + V_HEAD) + j, hd * HEAD_SLOT + j] = 1.0
        for j in range(V_HEAD):
            place_kv[hd * (QK_NOPE + V_HEAD) + QK_NOPE + j, QK_CAT + hd * V_HEAD + j] = 1.0
    place = lambda w, p: jnp.einsum("lqk,kn->lqn", w.astype(BF16), jnp.asarray(p, BF16), preferred_element_type=BF16)
    wq_rope = place(W["w_q_b"], place_q)
    wq_cat = wq_rope[:, :, :QK_CAT]
    wkv_cat = place(W["w_kv_b"], place_kv)

    eye = jnp.eye(POOL_GROUPS, dtype=F32)
    pool_bd = (eye[None, :, None, :, None] * W["pool_w"][:, :, :, None, :]).reshape(depth, POOL_W, POOL_W)

    row = lambda a: a.reshape(depth, 1, -1)
    stacked = {
        "pre_mix_g": row(W["pre_mix_g"]), "post_mix_g": row(W["post_mix_g"]),
        "pre_ffn_g": row(W["pre_ffn_g"]), "post_ffn_g": row(W["post_ffn_g"]),
        "w_small": w_small, "w_gate": w_gate, "b_gate": row(W["b_gate"]),
        "q_norm_g": row(W["q_norm_g"]), "wq": wq_cat.astype(BF16), "wq_rope": wq_rope.astype(BF16),
        "kv_norm_g": row(W["kv_norm_g"]), "wkv": wkv_cat.astype(BF16),
        "w_o_mla": W["w_o_mla"].astype(BF16),
        "conv_dw": W["conv_dw"], "conv_dw_b": row(W["conv_dw_b"]),
        "conv_ln_g": row(W["conv_ln_g"]), "conv_ln_b": row(W["conv_ln_b"]),
        "w_conv_out": W["w_conv_out"].astype(BF16),
        "gmlp_ln_g": row(W["gmlp_ln_g"]), "gmlp_ln_b": row(W["gmlp_ln_b"]),
        "gmlp_wcat": W["gmlp_ws"].transpose(0, 2, 1, 3).reshape(depth, GMLP_CHUNK, GMLP_GROUPS * GMLP_CHUNK).astype(BF16),
        "gmlp_bmat": jnp.repeat(W["gmlp_bs"].transpose(0, 2, 1), GMLP_W // GMLP_GROUPS, axis=2),
        "w_gmlp_out": W["w_gmlp_out"].astype(BF16),
        "pool_bd": pool_bd.astype(BF16),
        "pool_scale": row(W["pool_scale"]),
        "w_pool_out": W["w_pool_out"].astype(BF16),
        "w_out": W["w_out"].astype(BF16),
        "ffn_up": W["ffn_up"].astype(BF16),
        "ffn_dw": W["ffn_dw"],
        "ffn_dw_b": row(W["ffn_dw_b"]),
        "ffn_down": W["ffn_down"].astype(BF16),
    }
    return stacked, jnp.asarray(e, BF16)


def _trunk_layer(x2d, seq_len, mod3, mod_row_of_row, lw, rope_tabs, ctx):
    p = _rowproj_call(x2d, mod3, seq_len, mod_row_of_row, lw, rope_tabs)
    if ctx is None:
        fa = _attn_call([p["q"]], [p["k"]], [p["v"]], seq_len)
    else:
        k_ctx, v_ctx = ctx
        fa = _attn_call([p["q"], p["qp"]], [p["k"], k_ctx], [p["v"], v_ctx], seq_len)
    x1, h2 = _merge_call(x2d, mod3, mod_row_of_row, p["h"], fa, p["fb"], p["fc"], p["fd"], lw)
    x2 = _ffn_call(h2, x1, mod3, seq_len, mod_row_of_row, lw)
    return x2, p.get("ckv"), p.get("kpe")


def kernel(x_prompt, x_sample, cache_ckv, cache_kpe, c, c_ctx, ada_w, ada_b, pre_mix_g, post_mix_g, pre_ffn_g, post_ffn_g, w_in, b_gate, q_norm_g, w_q_b, kv_norm_g, w_kv_b, w_o_mla, conv_dw, conv_dw_b, conv_ln_g, conv_ln_b, w_conv_out, gmlp_ln_g, gmlp_ln_b, gmlp_ws, gmlp_bs, w_gmlp_out, pool_w, pool_scale, w_pool_out, w_out, ffn_up, ffn_dw, ffn_dw_b, ffn_down):
    W = dict(pre_mix_g=pre_mix_g, post_mix_g=post_mix_g, pre_ffn_g=pre_ffn_g, post_ffn_g=post_ffn_g,
             w_in=w_in, b_gate=b_gate, q_norm_g=q_norm_g, w_q_b=w_q_b, kv_norm_g=kv_norm_g,
             w_kv_b=w_kv_b, w_o_mla=w_o_mla, conv_dw=conv_dw, conv_dw_b=conv_dw_b,
             conv_ln_g=conv_ln_g, conv_ln_b=conv_ln_b, w_conv_out=w_conv_out, gmlp_ln_g=gmlp_ln_g,
             gmlp_ln_b=gmlp_ln_b, gmlp_ws=gmlp_ws, gmlp_bs=gmlp_bs, w_gmlp_out=w_gmlp_out,
             pool_w=pool_w, pool_scale=pool_scale, w_pool_out=w_pool_out, w_out=w_out,
             ffn_up=ffn_up, ffn_dw=ffn_dw, ffn_dw_b=ffn_dw_b, ffn_down=ffn_down)
    n_prompt, prompt_len, _ = x_prompt.shape
    n_sample, sample_len, _ = x_sample.shape
    past_len = cache_ckv.shape[2]
    ctx_row = n_sample

    cvec = jnp.zeros((SUBLANES, D_MODEL), F32).at[0:n_sample].set(c).at[ctx_row].set(c_ctx)
    mod = _mod_call(cvec, ada_w, ada_b)

    rope_tabs = _rope_tables(sample_len)
    xp = x_prompt.reshape(n_prompt * prompt_len, D_MODEL)
    xs = x_sample.reshape(n_sample * sample_len, D_MODEL)
    ckv_list, kpe_list = [], []
    stacked, e_kpe = _prep_weights(W)
    for l in range(DEPTH):
        mod3 = mod[l].reshape(SUBLANES, 1, 6 * D_MODEL)
        lw = _LayerWeights(stacked, e_kpe, l)
        xp, ckv_l, kpe_l = _trunk_layer(xp, prompt_len, mod3, lambda r: ctx_row, lw, None, None)
        ckv_list.append(ckv_l.reshape(n_prompt, prompt_len, KV_LORA))
        kpe_list.append(kpe_l.reshape(n_prompt, prompt_len, QK_ROPE))
        kpe_pad = jnp.pad(cache_kpe[:, l].reshape(n_sample * past_len, QK_ROPE),
                          ((0, 0), (0, LANES - QK_ROPE)))
        ctx = _ctxkv_call(cache_ckv[:, l].reshape(n_sample * past_len, KV_LORA), kpe_pad, lw)
        xs, _, _ = _trunk_layer(xs, sample_len, mod3, lambda r: r // sample_len, lw, rope_tabs, ctx)
    new_ckv = jnp.stack(ckv_list, axis=1)
    new_kpe = jnp.stack(kpe_list, axis=1)
    return (xp.reshape(n_prompt, prompt_len, D_MODEL), xs.reshape(n_sample, sample_len, D_MODEL),
            new_ckv, new_kpe)
```

```python
import functools

import jax
import jax.numpy as jnp
import numpy as np
from jax import lax
from jax.experimental import pallas as pl
from jax.experimental.pallas import tpu as pltpu

D_MODEL = 1024
DEPTH = 2
GRID_W = 64
MLA_HEADS = 8
Q_LORA = 384
KV_LORA = 256
QK_NOPE = 64
QK_ROPE = 32
V_HEAD = 64
MLA_W = MLA_HEADS * V_HEAD
SOFTMAX_SCALE = (QK_NOPE + QK_ROPE) ** -0.5
ROPE_BASE = 10000.0
CONV_W = 256
CONV_K = 31
GMLP_W = 256
GMLP_GROUPS = 4
GMLP_CHUNK = 128
POOL_W = 256
POOL_GROUPS = 4
N_BRANCH = 4
D_FF = 2816
FFN_K = 3
NORM_EPS = 1e-6
OFF_KV = Q_LORA
OFF_KPE = OFF_KV + KV_LORA
OFF_CONV = OFF_KPE + QK_ROPE
OFF_GMLP = OFF_CONV + 2 * CONV_W
OFF_POOL = OFF_GMLP + 2 * GMLP_W
OFF_GATE = OFF_POOL + POOL_W

LANES = 128
SUBLANES = 8
VMEM_LIMIT_BYTES = 56 * 1024 * 1024

S_Q = 0
S_KV = S_Q + Q_LORA
S_CONV = S_KV + KV_LORA
S_GMLP = S_CONV + 2 * CONV_W
S_POOL = S_GMLP + 2 * GMLP_W
S_KPE = S_POOL + POOL_W
S_KPEP = S_KPE + LANES
S_TOTAL = S_KPEP + LANES

HEAD_SLOT = LANES
QK_CAT = MLA_HEADS * HEAD_SLOT

ROW_TILE = 512
MERGE_ROWS = 256
FFN_TILE = 1024
FFN_CHUNK = 256
FFN_DOWN_ROWS = 256
FFN_NCHUNK = D_FF // FFN_CHUNK
SEQ_ROWS = 64
PAD_ROWS = 16
ATTN_Q_TILE = 256
ATTN_GROUP = 4

VEC_FIELDS = (("pre_mix_g", D_MODEL), ("post_mix_g", D_MODEL), ("pre_ffn_g", D_MODEL), ("post_ffn_g", D_MODEL),
              ("b_gate", N_BRANCH * D_MODEL), ("q_norm_g", Q_LORA), ("kv_norm_g", KV_LORA),
              ("conv_dw_b", CONV_W), ("conv_ln_g", CONV_W), ("conv_ln_b", CONV_W),
              ("gmlp_ln_g", GMLP_W), ("gmlp_ln_b", GMLP_W), ("pool_scale", POOL_W), ("ffn_dw_b", 2 * D_FF))
VEC_OFF = {}
for _name, _width in VEC_FIELDS:
    VEC_OFF[_name] = (sum(w for _, w in VEC_FIELDS[:len(VEC_OFF)]), _width)
VEC_TOTAL = sum(w for _, w in VEC_FIELDS)

BF16 = jnp.bfloat16
F32 = jnp.float32
INV_SQRT2 = 0.7071067811865476
LOG2E = 1.4426950408889634


def _dot(a, b):
    return jnp.dot(a, b, preferred_element_type=F32)


def _dot_nt(a, b):
    return lax.dot_general(a, b, (((1,), (1,)), ((), ())), preferred_element_type=F32)


def _rms(x, g):
    return x * lax.rsqrt(jnp.mean(x * x, axis=-1, keepdims=True) + NORM_EPS) * g


def _layernorm(x, g, b):
    mu = jnp.mean(x, axis=-1, keepdims=True)
    xc = x - mu
    var = jnp.mean(xc * xc, axis=-1, keepdims=True)
    return xc * lax.rsqrt(var + NORM_EPS) * g + b


def _sigmoid(x):
    return 1.0 / (1.0 + jnp.exp(-x))


def _vec(vec_ref, name, lo=0, width=None):
    start, full = VEC_OFF[name]
    return vec_ref[:, start + lo:start + lo + (full if width is None else width)]


def _const_spec(shape):
    nd = len(shape)
    return pl.BlockSpec(shape, lambda *_: (0,) * nd, pipeline_mode=pl.Buffered(1))


def _layer_spec(stacked, layer):
    tail = stacked.shape[1:]
    return pl.BlockSpec((None,) + tail, lambda *_: (layer,) + (0,) * len(tail), pipeline_mode=pl.Buffered(1))


def _params(n_axes, flags=None):
    return pltpu.CompilerParams(dimension_semantics=("arbitrary",) * n_axes,
                                vmem_limit_bytes=VMEM_LIMIT_BYTES, flags=flags)


def _mod_kernel(cv_ref, w_ref, b_ref, o_ref):
    cv = cv_ref[...]
    s = (cv * _sigmoid(cv)).astype(BF16)
    o_ref[0] = _dot(s, w_ref[0].astype(BF16)) + b_ref[0]


def _mod_call(cvec, ada_w, ada_b):
    tn = 1536
    n = 6 * D_MODEL
    return pl.pallas_call(
        _mod_kernel,
        grid=(DEPTH, n // tn),
        in_specs=[pl.BlockSpec((SUBLANES, D_MODEL), lambda l, j: (0, 0)),
                  pl.BlockSpec((1, D_MODEL, tn), lambda l, j: (l, 0, j)),
                  pl.BlockSpec((1, 1, tn), lambda l, j: (l, 0, j))],
        out_specs=pl.BlockSpec((1, SUBLANES, tn), lambda l, j: (l, 0, j)),
        out_shape=jax.ShapeDtypeStruct((DEPTH, SUBLANES, n), F32),
        compiler_params=_params(2),
        name="mod",
    )(cvec, ada_w, ada_b.reshape(DEPTH, 1, n))


def _rowproj_kernel(rope, seq_len, x_ref, mod_ref, e_ref, vec_ref, ws_ref, wq_ref, wkv_ref, gws_ref, gbm_ref,
                    cw_ref, pw_ref, *rest):
    rest = list(rest)
    if rope:
        ccat_ref, scat_ref, ckpe_ref, skpe_ref = rest[:4]
        rest = rest[4:]
    apad_ref, ppad_ref, pooled_ref = rest[-3:]
    rest = rest[:-3]
    h_ref, q_ref = rest[:2]
    rest = rest[2:]
    if rope:
        qp_ref = rest.pop(0)
    k_ref, v_ref = rest[:2]
    rest = rest[2:]
    if not rope:
        ckv_ref, kpe_ref = rest[:2]
        rest = rest[2:]
    fb_ref, fd_ref, fc_ref = rest
    x = x_ref[...]
    tm = x.shape[0]
    mod = mod_ref[...]
    sh1 = mod[:, 0:D_MODEL]
    sc1 = mod[:, D_MODEL:2 * D_MODEL]
    h = _rms(x, _vec(vec_ref, "pre_mix_g")) * (1.0 + sc1) + sh1
    hb = h.astype(BF16)
    h_ref[...] = hb

    ci = _dot(hb, ws_ref[:, S_CONV:S_GMLP])
    pin = _dot(hb, ws_ref[:, S_POOL:S_KPE])
    a = ci[:, 0:CONV_W] * _sigmoid(ci[:, CONV_W:2 * CONV_W])
    _seq_branches(seq_len, tm // seq_len, a, pin, cw_ref, vec_ref, fb_ref, apad_ref, ppad_ref, pooled_ref)
    qa = _dot(hb, ws_ref[:, S_Q:S_KV])
    kvc = _dot(hb, ws_ref[:, S_KV:S_CONV])
    gm = _dot(hb, ws_ref[:, S_GMLP:S_POOL])
    kpa = _dot(hb, ws_ref[:, S_KPE:S_KPEP])
    if rope:
        kpb = _dot(hb, ws_ref[:, S_KPEP:S_TOTAL])

    qn = _rms(qa, _vec(vec_ref, "q_norm_g")).astype(BF16)
    ckv = _rms(kvc, _vec(vec_ref, "kv_norm_g"))
    if not rope:
        ckv_ref[...] = ckv
        kpe_ref[...] = kpa[:, 0:QK_ROPE]
    kpr = kpa * ckpe_ref[...] + kpb * skpe_ref[...] if rope else kpa
    qq = _dot(qn, wq_ref[...]) * (SOFTMAX_SCALE * LOG2E)
    kvb = _dot(ckv.astype(BF16), wkv_ref[...])
    kpx = _dot(kpr.astype(BF16), e_ref[...])
    if rope:
        ccat = ccat_ref[...]
        scat = scat_ref[...]
        for hd in range(MLA_HEADS):
            lo = hd * HEAD_SLOT
            qpl = qq[:, lo:lo + HEAD_SLOT]
            qpr = qq[:, QK_CAT + lo:QK_CAT + lo + HEAD_SLOT]
            q_ref[:, lo:lo + HEAD_SLOT] = (qpl * ccat + qpr * scat).astype(BF16)
            qp_ref[:, lo:lo + HEAD_SLOT] = qpl.astype(BF16)
    else:
        q_ref[...] = qq.astype(BF16)
    k_ref[...] = (kvb[:, 0:QK_CAT] + kpx).astype(BF16)
    v_ref[...] = kvb[:, QK_CAT:QK_CAT + MLA_W].astype(BF16)

    uv = 0.5 * gm * (1.0 + lax.erf(gm * INV_SQRT2))
    u = uv[:, 0:GMLP_W]
    v = _layernorm(uv[:, GMLP_W:2 * GMLP_W], _vec(vec_ref, "gmlp_ln_g"), _vec(vec_ref, "gmlp_ln_b"))
    grp = lax.broadcasted_iota(jnp.int32, (GMLP_CHUNK, GMLP_W), 1) // (GMLP_W // GMLP_GROUPS)
    gws = gws_ref[...]
    gbm = gbm_ref[...]
    for n in range(tm // GMLP_CHUNK):
        r0 = n * GMLP_CHUNK
        vc = v[r0:r0 + GMLP_CHUNK]
        vs = jnp.concatenate([jnp.where(grp == g, vc, 0.0) for g in range(GMLP_GROUPS)], axis=0)
        sv = _dot(gws, vs.astype(BF16)) + gbm
        fc_ref[r0:r0 + GMLP_CHUNK, :] = (u[r0:r0 + GMLP_CHUNK] * sv).astype(BF16)
    fd_ref[...] = (_dot(pooled_ref[...].astype(BF16), pw_ref[...]) * _vec(vec_ref, "pool_scale")).astype(BF16)


def _rowproj_call(x2d, mod3, seq_len, mod_row_of_row, lw, rope_tabs):
    rows = x2d.shape[0]
    tm = max(ROW_TILE, seq_len)
    assert tm % seq_len == 0 and rows % tm == 0
    n_seq = tm // seq_len
    rope = rope_tabs is not None
    names = ["vecs", "w_small", "wq_rope" if rope else "wq", "wkv", "gmlp_wcat", "gmlp_bmat", "conv_dw", "pool_bd"]
    row_spec = lambda w: pl.BlockSpec((tm, w), lambda i: (i, 0))
    in_specs = [row_spec(D_MODEL),
                pl.BlockSpec((None, 1, 6 * D_MODEL), lambda i: (mod_row_of_row(i * tm), 0, 0)),
                _const_spec((LANES, QK_CAT))] + [_layer_spec(lw.stacked[n], lw.layer) for n in names]
    args = [x2d, mod3, lw.e_kpe] + [lw.stacked[n] for n in names]
    outs = [("h", D_MODEL, BF16), ("q", QK_CAT, BF16)]
    if rope:
        tiles_per_seq = seq_len // tm
        in_specs += [pl.BlockSpec((tm, LANES), lambda i: (i % tiles_per_seq, 0))] * 4
        args += list(rope_tabs)
        outs.append(("qp", QK_CAT, BF16))
    outs += [("k", QK_CAT, BF16), ("v", MLA_W, BF16)]
    if not rope:
        outs += [("ckv", KV_LORA, F32), ("kpe", QK_ROPE, F32)]
    outs += [("fb", CONV_W, BF16), ("fd", POOL_W, BF16), ("fc", GMLP_W, BF16)]
    span = seq_len + 2 * PAD_ROWS
    res = pl.pallas_call(
        functools.partial(_rowproj_kernel, rope, seq_len),
        grid=(rows // tm,),
        in_specs=in_specs,
        out_specs=[row_spec(w) for _, w, _ in outs],
        out_shape=[jax.ShapeDtypeStruct((rows, w), dt) for _, w, dt in outs],
        scratch_shapes=[pltpu.VMEM((n_seq * span, CONV_W), F32), pltpu.VMEM((n_seq * span, POOL_W), F32),
                        pltpu.VMEM((tm, POOL_W), F32)],
        compiler_params=_params(1),
        name="rowproj_rope" if rope else "rowproj",
    )(*args)
    return {name: r for (name, _, _), r in zip(outs, res)}


def _seq_branches(seq_len, n_seq, a, pin, cw_ref, vec_ref, fb_ref, apad_ref, ppad_ref, pooled_ref):
    L = seq_len
    span = L + 2 * PAD_ROWS
    zeros = jnp.zeros((PAD_ROWS, CONV_W), F32)
    for s in range(n_seq):
        for pad_ref, src in ((apad_ref, a), (ppad_ref, pin)):
            pad_ref[s * span:s * span + PAD_ROWS, :] = zeros
            pad_ref[s * span + L + PAD_ROWS:(s + 1) * span, :] = zeros
            pad_ref[s * span + PAD_ROWS:s * span + PAD_ROWS + L, :] = src[s * L:(s + 1) * L]
    cb = _vec(vec_ref, "conv_dw_b")
    clng = _vec(vec_ref, "conv_ln_g")
    clnb = _vec(vec_ref, "conv_ln_b")
    half_conv = CONV_K // 2
    lane = lax.broadcasted_iota(jnp.int32, (SEQ_ROWS, POOL_W), 1)
    grp_w = POOL_W // POOL_GROUPS
    half = jnp.where(lane < grp_w, 1, jnp.where(lane < 2 * grp_w, 2, jnp.where(lane < 3 * grp_w, 4, 8)))
    win = SEQ_ROWS + 2 * PAD_ROWS

    def shifted_windows(pad_ref, p0):
        w = pad_ref[p0:p0 + win, :]
        return [w] + [pltpu.roll(w, win - b, 0) for b in range(1, SUBLANES)]

    def tap_of(xs, off):
        a0 = off // SUBLANES * SUBLANES
        return xs[off % SUBLANES][a0:a0 + SEQ_ROWS]

    for s in range(n_seq):
        for r0 in range(0, L, SEQ_ROWS):
            out_rows = slice(s * L + r0, s * L + r0 + SEQ_ROWS)
            xs = shifted_windows(apad_ref, s * span + r0)
            acc = jnp.zeros((SEQ_ROWS, CONV_W), F32) + cb
            for k in range(CONV_K):
                acc = acc + tap_of(xs, PAD_ROWS + k - half_conv) * cw_ref[pl.ds(k, 1), :]
            y = _layernorm(acc, clng, clnb)
            fb_ref[out_rows, :] = (y * _sigmoid(y)).astype(BF16)

            ps = shifted_windows(ppad_ref, s * span + r0)
            tap = lambda j: tap_of(ps, PAD_ROWS + j)
            x0 = tap(0)
            s2 = tap(-1) + x0
            s4 = s2 + tap(-2) + tap(1)
            s8 = s4 + tap(-4) + tap(-3) + tap(2) + tap(3)
            s16 = s8 + tap(-8) + tap(-7) + tap(-6) + tap(-5) + tap(4) + tap(5) + tap(6) + tap(7)
            ssum = jnp.where(lane < grp_w, s2, jnp.where(lane < 2 * grp_w, s4, jnp.where(lane < 3 * grp_w, s8, s16)))
            t = r0 + lax.broadcasted_iota(jnp.int32, (SEQ_ROWS, POOL_W), 0)
            cnt = jnp.minimum(t + half, L) - jnp.maximum(t - half, 0)
            pooled_ref[out_rows, :] = ssum / cnt.astype(F32) - x0


def _ctxkv_kernel(ckv_ref, kpe_ref, wkv_ref, e_ref, k_ref, v_ref):
    kvb = _dot(ckv_ref[...].astype(BF16), wkv_ref[...])
    k_ref[...] = (kvb[:, 0:QK_CAT] + _dot(kpe_ref[...].astype(BF16), e_ref[...])).astype(BF16)
    v_ref[...] = kvb[:, QK_CAT:QK_CAT + MLA_W].astype(BF16)


def _ctxkv_call(ckv2d, kpe2d, lw):
    rows = ckv2d.shape[0]
    return pl.pallas_call(
        _ctxkv_kernel,
        grid=(1,),
        in_specs=[pl.BlockSpec((rows, KV_LORA), lambda i: (0, 0)),
                  pl.BlockSpec((rows, LANES), lambda i: (0, 0)),
                  _layer_spec(lw.stacked["wkv"], lw.layer), _const_spec((LANES, QK_CAT))],
        out_specs=[pl.BlockSpec((rows, QK_CAT), lambda i: (0, 0)), pl.BlockSpec((rows, MLA_W), lambda i: (0, 0))],
        out_shape=[jax.ShapeDtypeStruct((rows, QK_CAT), BF16), jax.ShapeDtypeStruct((rows, MLA_W), BF16)],
        compiler_params=_params(1),
        name="ctxkv",
    )(ckv2d, kpe2d, lw.stacked["wkv"], lw.e_kpe)


def _attn_kernel(nseg, group, q_rows, k_rows, *refs):
    q_refs = refs[0:nseg]
    k_refs = refs[nseg:2 * nseg]
    v_refs = refs[2 * nseg:3 * nseg]
    o_ref = refs[3 * nseg]
    out_lane = lax.broadcasted_iota(jnp.int32, (q_rows, LANES), 1)
    for g in range(group):
        qr = slice(g * q_rows, (g + 1) * q_rows)
        krs = [slice(g * n, (g + 1) * n) for n in k_rows]
        scores = [[_dot_nt(q_refs[s][qr, hd * HEAD_SLOT:(hd + 1) * HEAD_SLOT],
                           k_refs[s][krs[s], hd * HEAD_SLOT:(hd + 1) * HEAD_SLOT]) for s in range(nseg)]
                  for hd in range(MLA_HEADS)]
        probs = []
        for hd in range(MLA_HEADS):
            m = functools.reduce(jnp.maximum, [jnp.max(sc, axis=-1, keepdims=True) for sc in scores[hd]])
            probs.append([jnp.exp2(sc - m).astype(BF16) for sc in scores[hd]])
        for pair in range(MLA_HEADS // 2):
            plo = pair * LANES
            outs = []
            for hh in range(2):
                o = None
                for s in range(nseg):
                    vp = v_refs[s][krs[s], plo:plo + LANES]
                    lane = lax.broadcasted_iota(jnp.int32, vp.shape, 1)
                    own = (lane < V_HEAD) if hh == 0 else (lane >= V_HEAD)
                    t = _dot(probs[2 * pair + hh][s], jnp.where(own, vp, jnp.ones_like(vp)))
                    o = t if o is None else o + t
                outs.append(o * (1.0 / pltpu.roll(o, V_HEAD, 1)))
            o_ref[qr, plo:plo + LANES] = jnp.where(out_lane < V_HEAD, outs[0], outs[1]).astype(BF16)


def _attn_call(qs, ks, vs, seq_len):
    nseg = len(qs)
    rows = qs[0].shape[0]
    nb = rows // seq_len
    tq = min(ATTN_Q_TILE, seq_len)
    nq = seq_len // tq
    group = ATTN_GROUP if nq == 1 else 1
    k_rows = [k.shape[0] // nb for k in ks]
    q_spec = pl.BlockSpec((group * tq, QK_CAT), lambda b, j: (b * nq + j, 0))
    k_specs = [pl.BlockSpec((group * n, QK_CAT), lambda b, j: (b, 0)) for n in k_rows]
    v_specs = [pl.BlockSpec((group * n, MLA_W), lambda b, j: (b, 0)) for n in k_rows]
    return pl.pallas_call(
        functools.partial(_attn_kernel, nseg, group, tq, k_rows),
        grid=(nb // group, nq),
        in_specs=[q_spec] * nseg + k_specs + v_specs,
        out_specs=pl.BlockSpec((group * tq, MLA_W), lambda b, j: (b * nq + j, 0)),
        out_shape=jax.ShapeDtypeStruct((rows, MLA_W), BF16),
        compiler_params=_params(2),
        name="attn%d" % nseg,
    )(*qs, *ks, *vs)


def _merge_kernel(x_ref, mod_ref, h_ref, fa_ref, fb_ref, fc_ref, fd_ref, vec_ref, wg_ref,
                  wa_ref, wb_ref, wc_ref, wd_ref, wout_ref, x1_ref, h2_ref):
    tm = h_ref.shape[0]
    row_blocks = [slice(r0, r0 + MERGE_ROWS) for r0 in range(0, tm, MERGE_ROWS)]
    mod = mod_ref[...]
    g1 = mod[:, 2 * D_MODEL:3 * D_MODEL]
    sh2 = mod[:, 3 * D_MODEL:4 * D_MODEL]
    sc2 = mod[:, 4 * D_MODEL:5 * D_MODEL]
    merged = []
    for rs in row_blocks:
        hb = h_ref[rs, :]
        acc = None
        for k, (f_ref, w_ref) in enumerate(((fa_ref, wa_ref), (fb_ref, wb_ref), (fc_ref, wc_ref), (fd_ref, wd_ref))):
            cols = slice(k * D_MODEL, (k + 1) * D_MODEL)
            gate = _sigmoid(_dot(hb, wg_ref[:, cols]) + _vec(vec_ref, "b_gate", k * D_MODEL, D_MODEL))
            t = gate * _dot(f_ref[rs, :], w_ref[...])
            acc = t if acc is None else acc + t
        merged.append(acc.astype(BF16))
    mixes = [_dot(m, wout_ref[...]) for m in merged]
    for rs, mix in zip(row_blocks, mixes):
        x1 = x_ref[rs, :] + g1 * _rms(mix, _vec(vec_ref, "post_mix_g"))
        x1_ref[rs, :] = x1
        h2_ref[rs, :] = (_rms(x1, _vec(vec_ref, "pre_ffn_g")) * (1.0 + sc2) + sh2).astype(BF16)


def _merge_call(x2d, mod3, mod_row_of_row, h, fa, fb, fc, fd, lw):
    rows = x2d.shape[0]
    tm = ROW_TILE
    names = ["vecs", "w_gate", "w_o_mla", "w_conv_out", "w_gmlp_out", "w_pool_out", "w_out"]
    row_spec = lambda w: pl.BlockSpec((tm, w), lambda i: (i, 0))
    return pl.pallas_call(
        _merge_kernel,
        grid=(rows // tm,),
        in_specs=[row_spec(D_MODEL),
                  pl.BlockSpec((None, 1, 6 * D_MODEL), lambda i: (mod_row_of_row(i * tm), 0, 0)),
                  row_spec(D_MODEL), row_spec(MLA_W), row_spec(CONV_W), row_spec(GMLP_W), row_spec(POOL_W)]
                 + [_layer_spec(lw.stacked[n], lw.layer) for n in names],
        out_specs=[row_spec(D_MODEL), row_spec(D_MODEL)],
        out_shape=[jax.ShapeDtypeStruct((rows, D_MODEL), F32), jax.ShapeDtypeStruct((rows, D_MODEL), BF16)],
        compiler_params=_params(1),
        name="merge",
    )(x2d, mod3, h, fa, fb, fc, fd, *[lw.stacked[n] for n in names])


def _ffn_kernel(seq_len, h2_ref, x1_ref, mod_ref, vec_ref, wup_ref, dw_ref, wdn_ref, o_ref, act_ref):
    tm = h2_ref.shape[0]
    assert seq_len & (seq_len - 1) == 0
    pos = lax.broadcasted_iota(jnp.int32, (tm, FFN_CHUNK), 0) & (seq_len - 1)
    first = pos == 0
    last = pos == seq_len - 1

    def conv_cols(c0):
        cols = slice(c0, c0 + FFN_CHUNK)
        up = _dot(h2_ref[...], wup_ref[:, cols])
        dw = dw_ref[:, cols]
        prev = jnp.where(first, 0.0, pltpu.roll(up, 1, 0))
        nxt = jnp.where(last, 0.0, pltpu.roll(up, tm - 1, 0))
        return prev * dw[0:1, :] + up * dw[1:2, :] + nxt * dw[2:3, :] + _vec(vec_ref, "ffn_dw_b", c0, FFN_CHUNK)

    for c in range(FFN_NCHUNK):
        c0 = c * FFN_CHUNK
        g = conv_cols(c0)
        val = conv_cols(D_FF + c0)
        act_ref[:, c0:c0 + FFN_CHUNK] = (g * _sigmoid(g) * val).astype(BF16)
    g2 = mod_ref[...][:, 5 * D_MODEL:6 * D_MODEL]
    for r0 in range(0, tm, FFN_DOWN_ROWS):
        rs = slice(r0, r0 + FFN_DOWN_ROWS)
        f = _dot(act_ref[rs, :], wdn_ref[...])
        o_ref[rs, :] = x1_ref[rs, :] + g2 * _rms(f, _vec(vec_ref, "post_ffn_g"))


def _ffn_call(h2, x1, mod3, seq_len, mod_row_of_row, lw):
    rows = h2.shape[0]
    tm = max(FFN_TILE, seq_len)
    names = ["vecs", "ffn_up", "ffn_dw", "ffn_down"]
    row_spec = pl.BlockSpec((tm, D_MODEL), lambda i: (i, 0))
    return pl.pallas_call(
        functools.partial(_ffn_kernel, seq_len),
        grid=(rows // tm,),
        in_specs=[row_spec, row_spec,
                  pl.BlockSpec((None, 1, 6 * D_MODEL), lambda i: (mod_row_of_row(i * tm), 0, 0))]
                 + [_layer_spec(lw.stacked[n], lw.layer) for n in names],
        out_specs=row_spec,
        out_shape=jax.ShapeDtypeStruct((rows, D_MODEL), F32),
        scratch_shapes=[pltpu.VMEM((tm, D_FF), BF16)],
        compiler_params=_params(1),
        name="ffn",
    )(h2, x1, mod3, *[lw.stacked[n] for n in names])


def _rope_tables(seq_len):
    quarter = QK_ROPE // 4
    inv = ROPE_BASE ** (-jnp.arange(quarter, dtype=F32) / quarter)
    t = jnp.arange(seq_len)
    row = (t // GRID_W).astype(F32)[:, None] * inv[None, :]
    col = (t % GRID_W).astype(F32)[:, None] * inv[None, :]
    ang = jnp.concatenate([row, row, col, col], axis=1)
    cos, sin = jnp.cos(ang), jnp.sin(ang)
    ones = jnp.ones((seq_len, QK_NOPE), F32)
    zeros_n = jnp.zeros((seq_len, QK_NOPE), F32)
    zeros_t = jnp.zeros((seq_len, HEAD_SLOT - QK_NOPE - QK_ROPE), F32)
    ccat = jnp.concatenate([ones, cos, zeros_t], axis=1)
    scat = jnp.concatenate([zeros_n, sin, zeros_t], axis=1)
    zeros_k = jnp.zeros((seq_len, LANES - QK_ROPE), F32)
    ckpe = jnp.concatenate([cos, zeros_k], axis=1)
    skpe = jnp.concatenate([sin, zeros_k], axis=1)
    return ccat, scat, ckpe, skpe


class _LayerWeights:
    def __init__(self, stacked, e_kpe, layer):
        self.stacked = stacked
        self.e_kpe = e_kpe
        self.layer = layer


PREP_COLS = 256
N_GATE_STEPS = N_BRANCH * D_MODEL // PREP_COLS


def _split_w_in_kernel(wt_ref, small_ref, gate_ref):
    j = pl.program_id(1)

    @pl.when(j < N_GATE_STEPS)
    def _():
        r0 = pl.multiple_of(OFF_GATE + j * PREP_COLS, SUBLANES)
        gate_ref[...] = wt_ref[pl.ds(r0, PREP_COLS), :].T.astype(BF16)

    @pl.when(j == N_GATE_STEPS)
    def _():
        def put(c0, rows_t):
            small_ref[:, c0:c0 + rows_t.shape[0]] = rows_t.T.astype(BF16)

        for r0 in range(0, OFF_KPE, LANES):
            put(S_Q + r0, wt_ref[r0:r0 + LANES, :])
        for r0 in range(OFF_CONV, OFF_GATE, LANES):
            put(S_CONV + r0 - OFF_CONV, wt_ref[r0:r0 + LANES, :])
        kpe = wt_ref[OFF_KPE:OFF_CONV, :]
        quarter = QK_ROPE // 4
        rot = []
        for b0 in range(0, QK_ROPE, 2 * quarter):
            rot += [-kpe[b0 + quarter:b0 + 2 * quarter], kpe[b0:b0 + quarter]]
        zeros = jnp.zeros((LANES - QK_ROPE, D_MODEL), F32)
        put(S_KPE, jnp.concatenate([kpe, zeros], axis=0))
        put(S_KPEP, jnp.concatenate(rot + [zeros], axis=0))


def _split_w_in_call(w_in):
    depth, d_model, cols = w_in.shape
    wt = jnp.swapaxes(w_in, 1, 2)
    last = N_GATE_STEPS - 1
    return pl.pallas_call(
        _split_w_in_kernel,
        grid=(depth, N_GATE_STEPS + 1),
        in_specs=[pl.BlockSpec((None, cols, d_model), lambda l, j: (l, 0, 0), pipeline_mode=pl.Buffered(1))],
        out_specs=[pl.BlockSpec((None, d_model, S_TOTAL), lambda l, j: (l, 0, 0)),
                   pl.BlockSpec((None, d_model, PREP_COLS), lambda l, j: (l, 0, jnp.minimum(j, last)))],
        out_shape=[jax.ShapeDtypeStruct((depth, d_model, S_TOTAL), BF16),
                   jax.ShapeDtypeStruct((depth, d_model, N_BRANCH * D_MODEL), BF16)],
        compiler_params=_params(2),
        name="split_w_in",
    )(wt)


def _prep_weights(W):
    depth = W["w_in"].shape[0]
    w_small, w_gate = _split_w_in_call(W["w_in"])

    quarter = QK_ROPE // 4
    q_head = QK_NOPE + QK_ROPE
    place_q = np.zeros((MLA_HEADS * q_head, 2 * QK_CAT), np.float32)
    place_kv = np.zeros((MLA_HEADS * (QK_NOPE + V_HEAD), QK_CAT + MLA_W), np.float32)
    e = np.zeros((LANES, QK_CAT), np.float32)
    for hd in range(MLA_HEADS):
        for j in range(q_head):
            place_q[hd * q_head + j, hd * HEAD_SLOT + j] = 1.0
        for d in range(QK_ROPE):
            first_half = d % (2 * quarter) < quarter
            src = d + quarter if first_half else d - quarter
            place_q[hd * q_head + QK_NOPE + src, QK_CAT + hd * HEAD_SLOT + QK_NOPE + d] = -1.0 if first_half else 1.0
            e[d, hd * HEAD_SLOT + QK_NOPE + d] = 1.0
        for j in range(QK_NOPE):
            place_kv[hd * (QK_NOPE + V_HEAD) + j, hd * HEAD_SLOT + j] = 1.0
        for j in range(V_HEAD):
            place_kv[hd * (QK_NOPE + V_HEAD) + QK_NOPE + j, QK_CAT + hd * V_HEAD + j] = 1.0
    place = lambda w, p: jnp.einsum("lqk,kn->lqn", w.astype(BF16), jnp.asarray(p, BF16), preferred_element_type=BF16)
    wq_rope = place(W["w_q_b"], place_q)
    wq_cat = wq_rope[:, :, :QK_CAT]
    wkv_cat = place(W["w_kv_b"], place_kv)

    eye = jnp.eye(POOL_GROUPS, dtype=F32)
    pool_bd = (eye[None, :, None, :, None] * W["pool_w"][:, :, :, None, :]).reshape(depth, POOL_W, POOL_W)

    vecs = jnp.concatenate([W[name] for name, _ in VEC_FIELDS], axis=1).reshape(depth, 1, VEC_TOTAL)
    stacked = {
        "vecs": vecs, "w_small": w_small, "w_gate": w_gate,
        "wq": wq_cat, "wq_rope": wq_rope, "wkv": wkv_cat,
        "w_o_mla": W["w_o_mla"].astype(BF16),
        "conv_dw": W["conv_dw"],
        "w_conv_out": W["w_conv_out"].astype(BF16),
        "gmlp_wcat": W["gmlp_ws"].transpose(0, 2, 1, 3).reshape(depth, GMLP_CHUNK, GMLP_GROUPS * GMLP_CHUNK).astype(BF16),
        "gmlp_bmat": jnp.repeat(W["gmlp_bs"].transpose(0, 2, 1), GMLP_W // GMLP_GROUPS, axis=2),
        "w_gmlp_out": W["w_gmlp_out"].astype(BF16),
        "pool_bd": pool_bd.astype(BF16),
        "w_pool_out": W["w_pool_out"].astype(BF16),
        "w_out": W["w_out"].astype(BF16),
        "ffn_up": W["ffn_up"].astype(BF16),
        "ffn_dw": W["ffn_dw"],
        "ffn_down": W["ffn_down"].astype(BF16),
    }
    return stacked, jnp.asarray(e, BF16)


def _trunk_layer(x2d, seq_len, mod3, mod_row_of_row, lw, rope_tabs, ctx):
    p = _rowproj_call(x2d, mod3, seq_len, mod_row_of_row, lw, rope_tabs)
    if ctx is None:
        fa = _attn_call([p["q"]], [p["k"]], [p["v"]], seq_len)
    else:
        k_ctx, v_ctx = ctx
        fa = _attn_call([p["q"], p["qp"]], [p["k"], k_ctx], [p["v"], v_ctx], seq_len)
    x1, h2 = _merge_call(x2d, mod3, mod_row_of_row, p["h"], fa, p["fb"], p["fc"], p["fd"], lw)
    x2 = _ffn_call(h2, x1, mod3, seq_len, mod_row_of_row, lw)
    return x2, p.get("ckv"), p.get("kpe")


def kernel(x_prompt, x_sample, cache_ckv, cache_kpe, c, c_ctx, ada_w, ada_b, pre_mix_g, post_mix_g, pre_ffn_g, post_ffn_g, w_in, b_gate, q_norm_g, w_q_b, kv_norm_g, w_kv_b, w_o_mla, conv_dw, conv_dw_b, conv_ln_g, conv_ln_b, w_conv_out, gmlp_ln_g, gmlp_ln_b, gmlp_ws, gmlp_bs, w_gmlp_out, pool_w, pool_scale, w_pool_out, w_out, ffn_up, ffn_dw, ffn_dw_b, ffn_down):
    W = dict(pre_mix_g=pre_mix_g, post_mix_g=post_mix_g, pre_ffn_g=pre_ffn_g, post_ffn_g=post_ffn_g,
             w_in=w_in, b_gate=b_gate, q_norm_g=q_norm_g, w_q_b=w_q_b, kv_norm_g=kv_norm_g,
             w_kv_b=w_kv_b, w_o_mla=w_o_mla, conv_dw=conv_dw, conv_dw_b=conv_dw_b,
             conv_ln_g=conv_ln_g, conv_ln_b=conv_ln_b, w_conv_out=w_conv_out, gmlp_ln_g=gmlp_ln_g,
             gmlp_ln_b=gmlp_ln_b, gmlp_ws=gmlp_ws, gmlp_bs=gmlp_bs, w_gmlp_out=w_gmlp_out,
             pool_w=pool_w, pool_scale=pool_scale, w_pool_out=w_pool_out, w_out=w_out,
             ffn_up=ffn_up, ffn_dw=ffn_dw, ffn_dw_b=ffn_dw_b, ffn_down=ffn_down)
    n_prompt, prompt_len, _ = x_prompt.shape
    n_sample, sample_len, _ = x_sample.shape
    past_len = cache_ckv.shape[2]
    ctx_row = n_sample

    cvec = jnp.zeros((SUBLANES, D_MODEL), F32).at[0:n_sample].set(c).at[ctx_row].set(c_ctx)
    mod = _mod_call(cvec, ada_w, ada_b)

    rope_tabs = _rope_tables(sample_len)
    xp = x_prompt.reshape(n_prompt * prompt_len, D_MODEL)
    xs = x_sample.reshape(n_sample * sample_len, D_MODEL)
    ckv_list, kpe_list = [], []
    stacked, e_kpe = _prep_weights(W)
    mod3 = mod.reshape(DEPTH * SUBLANES, 1, 6 * D_MODEL)
    for l in range(DEPTH):
        lw = _LayerWeights(stacked, e_kpe, l)
        xp, ckv_l, kpe_l = _trunk_layer(xp, prompt_len, mod3, lambda r, l=l: l * SUBLANES + ctx_row, lw, None, None)
        ckv_list.append(ckv_l.reshape(n_prompt, prompt_len, KV_LORA))
        kpe_list.append(kpe_l.reshape(n_prompt, prompt_len, QK_ROPE))
        kpe_pad = jnp.pad(cache_kpe[:, l].reshape(n_sample * past_len, QK_ROPE),
                          ((0, 0), (0, LANES - QK_ROPE)))
        ctx = _ctxkv_call(cache_ckv[:, l].reshape(n_sample * past_len, KV_LORA), kpe_pad, lw)
        xs, _, _ = _trunk_layer(xs, sample_len, mod3, lambda r, l=l: l * SUBLANES + r // sample_len,
                                lw, rope_tabs, ctx)
    new_ckv = jnp.stack(ckv_list, axis=1)
    new_kpe = jnp.stack(kpe_list, axis=1)
    return (xp.reshape(n_prompt, prompt_len, D_MODEL), xs.reshape(n_sample, sample_len, D_MODEL),
            new_ckv, new_kpe)
```

```python
import functools

import jax
import jax.numpy as jnp
import numpy as np
from jax import lax
from jax.experimental import pallas as pl
from jax.experimental.pallas import tpu as pltpu

D_MODEL = 1024
DEPTH = 2
GRID_W = 64
MLA_HEADS = 8
Q_LORA = 384
KV_LORA = 256
QK_NOPE = 64
QK_ROPE = 32
V_HEAD = 64
MLA_W = MLA_HEADS * V_HEAD
SOFTMAX_SCALE = (QK_NOPE + QK_ROPE) ** -0.5
ROPE_BASE = 10000.0
CONV_W = 256
CONV_K = 31
GMLP_W = 256
GMLP_GROUPS = 4
GMLP_CHUNK = 128
POOL_W = 256
POOL_GROUPS = 4
N_BRANCH = 4
D_FF = 2816
FFN_K = 3
NORM_EPS = 1e-6
OFF_KV = Q_LORA
OFF_KPE = OFF_KV + KV_LORA
OFF_CONV = OFF_KPE + QK_ROPE
OFF_GMLP = OFF_CONV + 2 * CONV_W
OFF_POOL = OFF_GMLP + 2 * GMLP_W
OFF_GATE = OFF_POOL + POOL_W

LANES = 128
SUBLANES = 8
VMEM_LIMIT_BYTES = 56 * 1024 * 1024

S_Q = 0
S_KV = S_Q + Q_LORA
S_CONV = S_KV + KV_LORA
S_GMLP = S_CONV + 2 * CONV_W
S_POOL = S_GMLP + 2 * GMLP_W
S_KPE = S_POOL + POOL_W
S_KPEP = S_KPE + LANES
S_TOTAL = S_KPEP + LANES

HEAD_SLOT = LANES
QK_CAT = MLA_HEADS * HEAD_SLOT

ROW_TILE = 512
MERGE_ROWS = 256
FFN_TILE = 1024
FFN_CHUNK = 256
FFN_NCHUNK = D_FF // FFN_CHUNK
SEQ_ROWS = 64
PAD_ROWS = 16
ATTN_Q_TILE = 512
ATTN_GROUP = 4

VEC_FIELDS = (("pre_mix_g", D_MODEL), ("post_mix_g", D_MODEL), ("pre_ffn_g", D_MODEL), ("post_ffn_g", D_MODEL),
              ("b_gate", N_BRANCH * D_MODEL), ("q_norm_g", Q_LORA), ("kv_norm_g", KV_LORA),
              ("conv_dw_b", CONV_W), ("conv_ln_g", CONV_W), ("conv_ln_b", CONV_W),
              ("gmlp_ln_g", GMLP_W), ("gmlp_ln_b", GMLP_W), ("pool_scale", POOL_W), ("ffn_dw_b", 2 * D_FF))
VEC_OFF = {}
for _name, _width in VEC_FIELDS:
    VEC_OFF[_name] = (sum(w for _, w in VEC_FIELDS[:len(VEC_OFF)]), _width)
VEC_TOTAL = sum(w for _, w in VEC_FIELDS)

BF16 = jnp.bfloat16
F32 = jnp.float32
INV_SQRT2 = 0.7071067811865476
LOG2E = 1.4426950408889634


def _dot(a, b):
    return jnp.dot(a, b, preferred_element_type=F32)


def _dot_nt(a, b):
    return lax.dot_general(a, b, (((1,), (1,)), ((), ())), preferred_element_type=F32)


def _rms(x, g):
    return x * lax.rsqrt(jnp.mean(x * x, axis=-1, keepdims=True) + NORM_EPS) * g


def _layernorm(x, g, b):
    mu = jnp.mean(x, axis=-1, keepdims=True)
    xc = x - mu
    var = jnp.mean(xc * xc, axis=-1, keepdims=True)
    return xc * lax.rsqrt(var + NORM_EPS) * g + b


def _sigmoid(x):
    return 1.0 / (1.0 + jnp.exp(-x))


def _vec(vec_ref, name, lo=0, width=None):
    start, full = VEC_OFF[name]
    return vec_ref[:, start + lo:start + lo + (full if width is None else width)]


def _const_spec(shape):
    nd = len(shape)
    return pl.BlockSpec(shape, lambda *_: (0,) * nd, pipeline_mode=pl.Buffered(1))


def _layer_spec(stacked, layer):
    tail = stacked.shape[1:]
    return pl.BlockSpec((None,) + tail, lambda *_: (layer,) + (0,) * len(tail), pipeline_mode=pl.Buffered(1))


def _params(n_axes, flags=None):
    return pltpu.CompilerParams(dimension_semantics=("arbitrary",) * n_axes,
                                vmem_limit_bytes=VMEM_LIMIT_BYTES, flags=flags)


def _mod_kernel(cv_ref, w_ref, b_ref, o_ref):
    cv = cv_ref[...]
    s = (cv * _sigmoid(cv)).astype(BF16)
    o_ref[0] = _dot(s, w_ref[0].astype(BF16)) + b_ref[0]


def _mod_call(cvec, ada_w, ada_b):
    tn = 1536
    n = 6 * D_MODEL
    return pl.pallas_call(
        _mod_kernel,
        grid=(DEPTH, n // tn),
        in_specs=[pl.BlockSpec((SUBLANES, D_MODEL), lambda l, j: (0, 0)),
                  pl.BlockSpec((1, D_MODEL, tn), lambda l, j: (l, 0, j)),
                  pl.BlockSpec((1, 1, tn), lambda l, j: (l, 0, j))],
        out_specs=pl.BlockSpec((1, SUBLANES, tn), lambda l, j: (l, 0, j)),
        out_shape=jax.ShapeDtypeStruct((DEPTH, SUBLANES, n), F32),
        compiler_params=_params(2),
        name="mod",
    )(cvec, ada_w, ada_b.reshape(DEPTH, 1, n))


def _rowproj_kernel(rope, seq_len, x_ref, mod_ref, e_ref, vec_ref, ws_ref, wq_ref, wkv_ref, gws_ref, gbm_ref,
                    cw_ref, pw_ref, *rest):
    rest = list(rest)
    if rope:
        ccat_ref, scat_ref, ckpe_ref, skpe_ref = rest[:4]
        rest = rest[4:]
    apad_ref, ppad_ref, pooled_ref = rest[-3:]
    rest = rest[:-3]
    h_ref, q_ref = rest[:2]
    rest = rest[2:]
    if rope:
        qp_ref = rest.pop(0)
    k_ref, v_ref = rest[:2]
    rest = rest[2:]
    if not rope:
        ckv_ref, kpe_ref = rest[:2]
        rest = rest[2:]
    fb_ref, fd_ref, fc_ref = rest
    x = x_ref[...]
    tm = x.shape[0]
    mod = mod_ref[...]
    sh1 = mod[:, 0:D_MODEL]
    sc1 = mod[:, D_MODEL:2 * D_MODEL]
    h = _rms(x, _vec(vec_ref, "pre_mix_g")) * (1.0 + sc1) + sh1
    hb = h.astype(BF16)
    h_ref[...] = hb

    ci = _dot(hb, ws_ref[:, S_CONV:S_GMLP])
    pin = _dot(hb, ws_ref[:, S_POOL:S_KPE])
    a = ci[:, 0:CONV_W] * _sigmoid(ci[:, CONV_W:2 * CONV_W])
    _seq_branches(seq_len, tm // seq_len, a, pin, cw_ref, vec_ref, fb_ref, apad_ref, ppad_ref, pooled_ref)
    qa = _dot(hb, ws_ref[:, S_Q:S_KV])
    kvc = _dot(hb, ws_ref[:, S_KV:S_CONV])
    gm = _dot(hb, ws_ref[:, S_GMLP:S_POOL])
    kpa = _dot(hb, ws_ref[:, S_KPE:S_KPEP])
    if rope:
        kpb = _dot(hb, ws_ref[:, S_KPEP:S_TOTAL])

    qn = _rms(qa, _vec(vec_ref, "q_norm_g")).astype(BF16)
    ckv = _rms(kvc, _vec(vec_ref, "kv_norm_g"))
    if not rope:
        ckv_ref[...] = ckv
        kpe_ref[...] = kpa[:, 0:QK_ROPE]
    kpr = kpa * ckpe_ref[...] + kpb * skpe_ref[...] if rope else kpa
    qq = _dot(qn, wq_ref[...]) * (SOFTMAX_SCALE * LOG2E)
    kvb = _dot(ckv.astype(BF16), wkv_ref[...])
    kpx = _dot(kpr.astype(BF16), e_ref[...])
    if rope:
        ccat = ccat_ref[...]
        scat = scat_ref[...]
        for hd in range(MLA_HEADS):
            lo = hd * HEAD_SLOT
            qpl = qq[:, lo:lo + HEAD_SLOT]
            qpr = qq[:, QK_CAT + lo:QK_CAT + lo + HEAD_SLOT]
            q_ref[:, lo:lo + HEAD_SLOT] = (qpl * ccat + qpr * scat).astype(BF16)
            qp_ref[:, lo:lo + HEAD_SLOT] = qpl.astype(BF16)
    else:
        q_ref[...] = qq.astype(BF16)
    k_ref[...] = (kvb[:, 0:QK_CAT] + kpx).astype(BF16)
    v_ref[...] = kvb[:, QK_CAT:QK_CAT + MLA_W].astype(BF16)

    uv = 0.5 * gm * (1.0 + lax.erf(gm * INV_SQRT2))
    u = uv[:, 0:GMLP_W]
    v = _layernorm(uv[:, GMLP_W:2 * GMLP_W], _vec(vec_ref, "gmlp_ln_g"), _vec(vec_ref, "gmlp_ln_b"))
    grp = lax.broadcasted_iota(jnp.int32, (GMLP_CHUNK, GMLP_W), 1) // (GMLP_W // GMLP_GROUPS)
    gws = gws_ref[...]
    gbm = gbm_ref[...]
    for n in range(tm // GMLP_CHUNK):
        r0 = n * GMLP_CHUNK
        vc = v[r0:r0 + GMLP_CHUNK]
        vs = jnp.concatenate([jnp.where(grp == g, vc, 0.0) for g in range(GMLP_GROUPS)], axis=0)
        sv = _dot(gws, vs.astype(BF16)) + gbm
        fc_ref[r0:r0 + GMLP_CHUNK, :] = (u[r0:r0 + GMLP_CHUNK] * sv).astype(BF16)
    fd_ref[...] = (_dot(pooled_ref[...].astype(BF16), pw_ref[...]) * _vec(vec_ref, "pool_scale")).astype(BF16)


def _rowproj_call(x2d, mod3, seq_len, mod_row_of_row, lw, rope_tabs):
    rows = x2d.shape[0]
    tm = max(ROW_TILE, seq_len)
    assert tm % seq_len == 0 and rows % tm == 0
    n_seq = tm // seq_len
    rope = rope_tabs is not None
    names = ["vecs", "w_small", "wq_rope" if rope else "wq", "wkv", "gmlp_wcat", "gmlp_bmat", "conv_dw", "pool_bd"]
    row_spec = lambda w: pl.BlockSpec((tm, w), lambda i: (i, 0))
    in_specs = [row_spec(D_MODEL),
                pl.BlockSpec((None, 1, 6 * D_MODEL), lambda i: (mod_row_of_row(i * tm), 0, 0)),
                _const_spec((LANES, QK_CAT))] + [_layer_spec(lw.stacked[n], lw.layer) for n in names]
    args = [x2d, mod3, lw.e_kpe] + [lw.stacked[n] for n in names]
    outs = [("h", D_MODEL, BF16), ("q", QK_CAT, BF16)]
    if rope:
        tiles_per_seq = seq_len // tm
        in_specs += [pl.BlockSpec((tm, LANES), lambda i: (i % tiles_per_seq, 0))] * 4
        args += list(rope_tabs)
        outs.append(("qp", QK_CAT, BF16))
    outs += [("k", QK_CAT, BF16), ("v", MLA_W, BF16)]
    if not rope:
        outs += [("ckv", KV_LORA, F32), ("kpe", QK_ROPE, F32)]
    outs += [("fb", CONV_W, BF16), ("fd", POOL_W, BF16), ("fc", GMLP_W, BF16)]
    span = seq_len + 2 * PAD_ROWS
    res = pl.pallas_call(
        functools.partial(_rowproj_kernel, rope, seq_len),
        grid=(rows // tm,),
        in_specs=in_specs,
        out_specs=[row_spec(w) for _, w, _ in outs],
        out_shape=[jax.ShapeDtypeStruct((rows, w), dt) for _, w, dt in outs],
        scratch_shapes=[pltpu.VMEM((n_seq * span, CONV_W), F32), pltpu.VMEM((n_seq * span, POOL_W), F32),
                        pltpu.VMEM((tm, POOL_W), F32)],
        compiler_params=_params(1),
        name="rowproj_rope" if rope else "rowproj",
    )(*args)
    return {name: r for (name, _, _), r in zip(outs, res)}


def _seq_branches(seq_len, n_seq, a, pin, cw_ref, vec_ref, fb_ref, apad_ref, ppad_ref, pooled_ref):
    L = seq_len
    span = L + 2 * PAD_ROWS
    zeros = jnp.zeros((PAD_ROWS, CONV_W), F32)
    for s in range(n_seq):
        for pad_ref, src in ((apad_ref, a), (ppad_ref, pin)):
            pad_ref[s * span:s * span + PAD_ROWS, :] = zeros
            pad_ref[s * span + L + PAD_ROWS:(s + 1) * span, :] = zeros
            pad_ref[s * span + PAD_ROWS:s * span + PAD_ROWS + L, :] = src[s * L:(s + 1) * L]
    cb = _vec(vec_ref, "conv_dw_b")
    clng = _vec(vec_ref, "conv_ln_g")
    clnb = _vec(vec_ref, "conv_ln_b")
    half_conv = CONV_K // 2
    lane = lax.broadcasted_iota(jnp.int32, (SEQ_ROWS, POOL_W), 1)
    grp_w = POOL_W // POOL_GROUPS
    half = jnp.where(lane < grp_w, 1, jnp.where(lane < 2 * grp_w, 2, jnp.where(lane < 3 * grp_w, 4, 8)))
    win = SEQ_ROWS + 2 * PAD_ROWS

    def shifted_windows(pad_ref, p0):
        w = pad_ref[p0:p0 + win, :]
        return [w] + [pltpu.roll(w, win - b, 0) for b in range(1, SUBLANES)]

    def tap_of(xs, off):
        a0 = off // SUBLANES * SUBLANES
        return xs[off % SUBLANES][a0:a0 + SEQ_ROWS]

    for s in range(n_seq):
        for r0 in range(0, L, SEQ_ROWS):
            out_rows = slice(s * L + r0, s * L + r0 + SEQ_ROWS)
            xs = shifted_windows(apad_ref, s * span + r0)
            acc = jnp.zeros((SEQ_ROWS, CONV_W), F32) + cb
            for k in range(CONV_K):
                acc = acc + tap_of(xs, PAD_ROWS + k - half_conv) * cw_ref[pl.ds(k, 1), :]
            y = _layernorm(acc, clng, clnb)
            fb_ref[out_rows, :] = (y * _sigmoid(y)).astype(BF16)

            ps = shifted_windows(ppad_ref, s * span + r0)
            tap = lambda j: tap_of(ps, PAD_ROWS + j)
            x0 = tap(0)
            s2 = tap(-1) + x0
            s4 = s2 + tap(-2) + tap(1)
            s8 = s4 + tap(-4) + tap(-3) + tap(2) + tap(3)
            s16 = s8 + tap(-8) + tap(-7) + tap(-6) + tap(-5) + tap(4) + tap(5) + tap(6) + tap(7)
            ssum = jnp.where(lane < grp_w, s2, jnp.where(lane < 2 * grp_w, s4, jnp.where(lane < 3 * grp_w, s8, s16)))
            t = r0 + lax.broadcasted_iota(jnp.int32, (SEQ_ROWS, POOL_W), 0)
            cnt = jnp.minimum(t + half, L) - jnp.maximum(t - half, 0)
            pooled_ref[out_rows, :] = ssum / cnt.astype(F32) - x0


def _ctxkv_kernel(ckv_ref, kpe_ref, wkv_ref, e_ref, k_ref, v_ref):
    kvb = _dot(ckv_ref[...].astype(BF16), wkv_ref[...])
    k_ref[...] = (kvb[:, 0:QK_CAT] + _dot(kpe_ref[...].astype(BF16), e_ref[...])).astype(BF16)
    v_ref[...] = kvb[:, QK_CAT:QK_CAT + MLA_W].astype(BF16)


def _ctxkv_call(ckv2d, kpe2d, lw):
    rows = ckv2d.shape[0]
    return pl.pallas_call(
        _ctxkv_kernel,
        grid=(1,),
        in_specs=[pl.BlockSpec((rows, KV_LORA), lambda i: (0, 0)),
                  pl.BlockSpec((rows, LANES), lambda i: (0, 0)),
                  _layer_spec(lw.stacked["wkv"], lw.layer), _const_spec((LANES, QK_CAT))],
        out_specs=[pl.BlockSpec((rows, QK_CAT), lambda i: (0, 0)), pl.BlockSpec((rows, MLA_W), lambda i: (0, 0))],
        out_shape=[jax.ShapeDtypeStruct((rows, QK_CAT), BF16), jax.ShapeDtypeStruct((rows, MLA_W), BF16)],
        compiler_params=_params(1),
        name="ctxkv",
    )(ckv2d, kpe2d, lw.stacked["wkv"], lw.e_kpe)


def _attn_kernel(nseg, group, q_rows, k_rows, *refs):
    q_refs = refs[0:nseg]
    k_refs = refs[nseg:2 * nseg]
    v_refs = refs[2 * nseg:3 * nseg]
    o_ref = refs[3 * nseg]
    out_lane = lax.broadcasted_iota(jnp.int32, (q_rows, LANES), 1)
    for g in range(group):
        qr = slice(g * q_rows, (g + 1) * q_rows)
        krs = [slice(g * n, (g + 1) * n) for n in k_rows]
        scores = [[_dot_nt(q_refs[s][qr, hd * HEAD_SLOT:(hd + 1) * HEAD_SLOT],
                           k_refs[s][krs[s], hd * HEAD_SLOT:(hd + 1) * HEAD_SLOT]) for s in range(nseg)]
                  for hd in range(MLA_HEADS)]
        probs = []
        for hd in range(MLA_HEADS):
            m = functools.reduce(jnp.maximum, [jnp.max(sc, axis=-1, keepdims=True) for sc in scores[hd]])
            probs.append([jnp.exp2(sc - m).astype(BF16) for sc in scores[hd]])
        for pair in range(MLA_HEADS // 2):
            plo = pair * LANES
            outs = []
            for hh in range(2):
                o = None
                for s in range(nseg):
                    vp = v_refs[s][krs[s], plo:plo + LANES]
                    lane = lax.broadcasted_iota(jnp.int32, vp.shape, 1)
                    own = (lane < V_HEAD) if hh == 0 else (lane >= V_HEAD)
                    t = _dot(probs[2 * pair + hh][s], jnp.where(own, vp, jnp.ones_like(vp)))
                    o = t if o is None else o + t
                outs.append(o * (1.0 / pltpu.roll(o, V_HEAD, 1)))
            o_ref[qr, plo:plo + LANES] = jnp.where(out_lane < V_HEAD, outs[0], outs[1]).astype(BF16)


def _attn_call(qs, ks, vs, seq_len):
    nseg = len(qs)
    rows = qs[0].shape[0]
    nb = rows // seq_len
    tq = min(ATTN_Q_TILE, seq_len)
    nq = seq_len // tq
    group = ATTN_GROUP if nq == 1 else 1
    k_rows = [k.shape[0] // nb for k in ks]
    q_spec = pl.BlockSpec((group * tq, QK_CAT), lambda b, j: (b * nq + j, 0))
    k_specs = [pl.BlockSpec((group * n, QK_CAT), lambda b, j: (b, 0)) for n in k_rows]
    v_specs = [pl.BlockSpec((group * n, MLA_W), lambda b, j: (b, 0)) for n in k_rows]
    return pl.pallas_call(
        functools.partial(_attn_kernel, nseg, group, tq, k_rows),
        grid=(nb // group, nq),
        in_specs=[q_spec] * nseg + k_specs + v_specs,
        out_specs=pl.BlockSpec((group * tq, MLA_W), lambda b, j: (b * nq + j, 0)),
        out_shape=jax.ShapeDtypeStruct((rows, MLA_W), BF16),
        compiler_params=_params(2),
        name="attn%d" % nseg,
    )(*qs, *ks, *vs)


def _merge_kernel(x_ref, mod_ref, h_ref, fa_ref, fb_ref, fc_ref, fd_ref, vec_ref, wg_ref,
                  wa_ref, wb_ref, wc_ref, wd_ref, wout_ref, x1_ref, h2_ref):
    tm = h_ref.shape[0]
    row_blocks = [slice(r0, r0 + MERGE_ROWS) for r0 in range(0, tm, MERGE_ROWS)]
    mod = mod_ref[...]
    g1 = mod[:, 2 * D_MODEL:3 * D_MODEL]
    sh2 = mod[:, 3 * D_MODEL:4 * D_MODEL]
    sc2 = mod[:, 4 * D_MODEL:5 * D_MODEL]
    merged = []
    for rs in row_blocks:
        hb = h_ref[rs, :]
        acc = None
        for k, (f_ref, w_ref) in enumerate(((fa_ref, wa_ref), (fb_ref, wb_ref), (fc_ref, wc_ref), (fd_ref, wd_ref))):
            cols = slice(k * D_MODEL, (k + 1) * D_MODEL)
            gate = _sigmoid(_dot(hb, wg_ref[:, cols]) + _vec(vec_ref, "b_gate", k * D_MODEL, D_MODEL))
            t = gate * _dot(f_ref[rs, :], w_ref[...])
            acc = t if acc is None else acc + t
        merged.append(acc.astype(BF16))
    mixes = [_dot(m, wout_ref[...]) for m in merged]
    for rs, mix in zip(row_blocks, mixes):
        x1 = x_ref[rs, :] + g1 * _rms(mix, _vec(vec_ref, "post_mix_g"))
        x1_ref[rs, :] = x1
        h2_ref[rs, :] = (_rms(x1, _vec(vec_ref, "pre_ffn_g")) * (1.0 + sc2) + sh2).astype(BF16)


def _merge_call(x2d, mod3, mod_row_of_row, h, fa, fb, fc, fd, lw):
    rows = x2d.shape[0]
    tm = ROW_TILE
    names = ["vecs", "w_gate", "w_o_mla", "w_conv_out", "w_gmlp_out", "w_pool_out", "w_out"]
    row_spec = lambda w: pl.BlockSpec((tm, w), lambda i: (i, 0))
    return pl.pallas_call(
        _merge_kernel,
        grid=(rows // tm,),
        in_specs=[row_spec(D_MODEL),
                  pl.BlockSpec((None, 1, 6 * D_MODEL), lambda i: (mod_row_of_row(i * tm), 0, 0)),
                  row_spec(D_MODEL), row_spec(MLA_W), row_spec(CONV_W), row_spec(GMLP_W), row_spec(POOL_W)]
                 + [_layer_spec(lw.stacked[n], lw.layer) for n in names],
        out_specs=[row_spec(D_MODEL), row_spec(D_MODEL)],
        out_shape=[jax.ShapeDtypeStruct((rows, D_MODEL), F32), jax.ShapeDtypeStruct((rows, D_MODEL), BF16)],
        compiler_params=_params(1),
        name="merge",
    )(x2d, mod3, h, fa, fb, fc, fd, *[lw.stacked[n] for n in names])


def _ffn_kernel(seq_len, h2_ref, x1_ref, mod_ref, vec_ref, wup_ref, dw_ref, wdn_ref, o_ref, act_ref):
    tm = h2_ref.shape[0]
    assert seq_len & (seq_len - 1) == 0
    pos = lax.broadcasted_iota(jnp.int32, (tm, FFN_CHUNK), 0) & (seq_len - 1)
    first = pos == 0
    last = pos == seq_len - 1

    def conv_cols(c0):
        cols = slice(c0, c0 + FFN_CHUNK)
        up = _dot(h2_ref[...], wup_ref[:, cols])
        dw = dw_ref[:, cols]
        prev = jnp.where(first, 0.0, pltpu.roll(up, 1, 0))
        nxt = jnp.where(last, 0.0, pltpu.roll(up, tm - 1, 0))
        return prev * dw[0:1, :] + up * dw[1:2, :] + nxt * dw[2:3, :] + _vec(vec_ref, "ffn_dw_b", c0, FFN_CHUNK)

    for c in range(FFN_NCHUNK):
        c0 = c * FFN_CHUNK
        g = conv_cols(c0)
        val = conv_cols(D_FF + c0)
        act_ref[:, c0:c0 + FFN_CHUNK] = (g * _sigmoid(g) * val).astype(BF16)
    f = _dot(act_ref[...], wdn_ref[...])
    g2 = mod_ref[...][:, 5 * D_MODEL:6 * D_MODEL]
    o_ref[...] = x1_ref[...] + g2 * _rms(f, _vec(vec_ref, "post_ffn_g"))


def _ffn_call(h2, x1, mod3, seq_len, mod_row_of_row, lw):
    rows = h2.shape[0]
    tm = max(FFN_TILE, seq_len)
    names = ["vecs", "ffn_up", "ffn_dw", "ffn_down"]
    row_spec = pl.BlockSpec((tm, D_MODEL), lambda i: (i, 0))
    return pl.pallas_call(
        functools.partial(_ffn_kernel, seq_len),
        grid=(rows // tm,),
        in_specs=[row_spec, row_spec,
                  pl.BlockSpec((None, 1, 6 * D_MODEL), lambda i: (mod_row_of_row(i * tm), 0, 0))]
                 + [_layer_spec(lw.stacked[n], lw.layer) for n in names],
        out_specs=row_spec,
        out_shape=jax.ShapeDtypeStruct((rows, D_MODEL), F32),
        scratch_shapes=[pltpu.VMEM((tm, D_FF), BF16)],
        compiler_params=_params(1),
        name="ffn",
    )(h2, x1, mod3, *[lw.stacked[n] for n in names])


def _rope_tables(seq_len):
    quarter = QK_ROPE // 4
    inv = ROPE_BASE ** (-jnp.arange(quarter, dtype=F32) / quarter)
    t = jnp.arange(seq_len)
    row = (t // GRID_W).astype(F32)[:, None] * inv[None, :]
    col = (t % GRID_W).astype(F32)[:, None] * inv[None, :]
    ang = jnp.concatenate([row, row, col, col], axis=1)
    cos, sin = jnp.cos(ang), jnp.sin(ang)
    ones = jnp.ones((seq_len, QK_NOPE), F32)
    zeros_n = jnp.zeros((seq_len, QK_NOPE), F32)
    zeros_t = jnp.zeros((seq_len, HEAD_SLOT - QK_NOPE - QK_ROPE), F32)
    ccat = jnp.concatenate([ones, cos, zeros_t], axis=1)
    scat = jnp.concatenate([zeros_n, sin, zeros_t], axis=1)
    zeros_k = jnp.zeros((seq_len, LANES - QK_ROPE), F32)
    ckpe = jnp.concatenate([cos, zeros_k], axis=1)
    skpe = jnp.concatenate([sin, zeros_k], axis=1)
    return ccat, scat, ckpe, skpe


class _LayerWeights:
    def __init__(self, stacked, e_kpe, layer):
        self.stacked = stacked
        self.e_kpe = e_kpe
        self.layer = layer


PREP_COLS = 256
N_GATE_STEPS = N_BRANCH * D_MODEL // PREP_COLS


def _split_w_in_kernel(wt_ref, small_ref, gate_ref):
    j = pl.program_id(1)

    @pl.when(j < N_GATE_STEPS)
    def _():
        r0 = pl.multiple_of(OFF_GATE + j * PREP_COLS, SUBLANES)
        gate_ref[...] = wt_ref[pl.ds(r0, PREP_COLS), :].T.astype(BF16)

    @pl.when(j == N_GATE_STEPS)
    def _():
        def put(c0, rows_t):
            small_ref[:, c0:c0 + rows_t.shape[0]] = rows_t.T.astype(BF16)

        for r0 in range(0, OFF_KPE, LANES):
            put(S_Q + r0, wt_ref[r0:r0 + LANES, :])
        for r0 in range(OFF_CONV, OFF_GATE, LANES):
            put(S_CONV + r0 - OFF_CONV, wt_ref[r0:r0 + LANES, :])
        kpe = wt_ref[OFF_KPE:OFF_CONV, :]
        quarter = QK_ROPE // 4
        rot = []
        for b0 in range(0, QK_ROPE, 2 * quarter):
            rot += [-kpe[b0 + quarter:b0 + 2 * quarter], kpe[b0:b0 + quarter]]
        zeros = jnp.zeros((LANES - QK_ROPE, D_MODEL), F32)
        put(S_KPE, jnp.concatenate([kpe, zeros], axis=0))
        put(S_KPEP, jnp.concatenate(rot + [zeros], axis=0))


def _split_w_in_call(w_in):
    depth, d_model, cols = w_in.shape
    wt = jnp.swapaxes(w_in, 1, 2)
    last = N_GATE_STEPS - 1
    return pl.pallas_call(
        _split_w_in_kernel,
        grid=(depth, N_GATE_STEPS + 1),
        in_specs=[pl.BlockSpec((None, cols, d_model), lambda l, j: (l, 0, 0), pipeline_mode=pl.Buffered(1))],
        out_specs=[pl.BlockSpec((None, d_model, S_TOTAL), lambda l, j: (l, 0, 0)),
                   pl.BlockSpec((None, d_model, PREP_COLS), lambda l, j: (l, 0, jnp.minimum(j, last)))],
        out_shape=[jax.ShapeDtypeStruct((depth, d_model, S_TOTAL), BF16),
                   jax.ShapeDtypeStruct((depth, d_model, N_BRANCH * D_MODEL), BF16)],
        compiler_params=_params(2),
        name="split_w_in",
    )(wt)


def _prep_weights(W):
    depth = W["w_in"].shape[0]
    w_small, w_gate = _split_w_in_call(W["w_in"])

    quarter = QK_ROPE // 4
    q_head = QK_NOPE + QK_ROPE
    place_q = np.zeros((MLA_HEADS * q_head, 2 * QK_CAT), np.float32)
    place_kv = np.zeros((MLA_HEADS * (QK_NOPE + V_HEAD), QK_CAT + MLA_W), np.float32)
    e = np.zeros((LANES, QK_CAT), np.float32)
    for hd in range(MLA_HEADS):
        for j in range(q_head):
            place_q[hd * q_head + j, hd * HEAD_SLOT + j] = 1.0
        for d in range(QK_ROPE):
            first_half = d % (2 * quarter) < quarter
            src = d + quarter if first_half else d - quarter
            place_q[hd * q_head + QK_NOPE + src, QK_CAT + hd * HEAD_SLOT + QK_NOPE + d] = -1.0 if first_half else 1.0
            e[d, hd * HEAD_SLOT + QK_NOPE + d] = 1.0
        for j in range(QK_NOPE):
            place_kv[hd * (QK_NOPE + V_HEAD) + j, hd * HEAD_SLOT + j] = 1.0
        for j in range(V_HEAD):
            place_kv[hd * (QK_NOPE + V_HEAD) + QK_NOPE + j, QK_CAT + hd * V_HEAD + j] = 1.0
    place = lambda w, p: jnp.einsum("lqk,kn->lqn", w.astype(BF16), jnp.asarray(p, BF16), preferred_element_type=BF16)
    wq_rope = place(W["w_q_b"], place_q)
    wq_cat = wq_rope[:, :, :QK_CAT]
    wkv_cat = place(W["w_kv_b"], place_kv)

    eye = jnp.eye(POOL_GROUPS, dtype=F32)
    pool_bd = (eye[None, :, None, :, None] * W["pool_w"][:, :, :, None, :]).reshape(depth, POOL_W, POOL_W)

    vecs = jnp.concatenate([W[name] for name, _ in VEC_FIELDS], axis=1).reshape(depth, 1, VEC_TOTAL)
    stacked = {
        "vecs": vecs, "w_small": w_small, "w_gate": w_gate,
        "wq": wq_cat, "wq_rope": wq_rope, "wkv": wkv_cat,
        "w_o_mla": W["w_o_mla"].astype(BF16),
        "conv_dw": W["conv_dw"],
        "w_conv_out": W["w_conv_out"].astype(BF16),
        "gmlp_wcat": W["gmlp_ws"].transpose(0, 2, 1, 3).reshape(depth, GMLP_CHUNK, GMLP_GROUPS * GMLP_CHUNK).astype(BF16),
        "gmlp_bmat": jnp.repeat(W["gmlp_bs"].transpose(0, 2, 1), GMLP_W // GMLP_GROUPS, axis=2),
        "w_gmlp_out": W["w_gmlp_out"].astype(BF16),
        "pool_bd": pool_bd.astype(BF16),
        "w_pool_out": W["w_pool_out"].astype(BF16),
        "w_out": W["w_out"].astype(BF16),
        "ffn_up": W["ffn_up"].astype(BF16),
        "ffn_dw": W["ffn_dw"],
        "ffn_down": W["ffn_down"].astype(BF16),
    }
    return stacked, jnp.asarray(e, BF16)


def _trunk_layer(x2d, seq_len, mod3, mod_row_of_row, lw, rope_tabs, ctx):
    p = _rowproj_call(x2d, mod3, seq_len, mod_row_of_row, lw, rope_tabs)
    if ctx is None:
        fa = _attn_call([p["q"]], [p["k"]], [p["v"]], seq_len)
    else:
        k_ctx, v_ctx = ctx
        fa = _attn_call([p["q"], p["qp"]], [p["k"], k_ctx], [p["v"], v_ctx], seq_len)
    x1, h2 = _merge_call(x2d, mod3, mod_row_of_row, p["h"], fa, p["fb"], p["fc"], p["fd"], lw)
    x2 = _ffn_call(h2, x1, mod3, seq_len, mod_row_of_row, lw)
    return x2, p.get("ckv"), p.get("kpe")


def kernel(x_prompt, x_sample, cache_ckv, cache_kpe, c, c_ctx, ada_w, ada_b, pre_mix_g, post_mix_g, pre_ffn_g, post_ffn_g, w_in, b_gate, q_norm_g, w_q_b, kv_norm_g, w_kv_b, w_o_mla, conv_dw, conv_dw_b, conv_ln_g, conv_ln_b, w_conv_out, gmlp_ln_g, gmlp_ln_b, gmlp_ws, gmlp_bs, w_gmlp_out, pool_w, pool_scale, w_pool_out, w_out, ffn_up, ffn_dw, ffn_dw_b, ffn_down):
    W = dict(pre_mix_g=pre_mix_g, post_mix_g=post_mix_g, pre_ffn_g=pre_ffn_g, post_ffn_g=post_ffn_g,
             w_in=w_in, b_gate=b_gate, q_norm_g=q_norm_g, w_q_b=w_q_b, kv_norm_g=kv_norm_g,
             w_kv_b=w_kv_b, w_o_mla=w_o_mla, conv_dw=conv_dw, conv_dw_b=conv_dw_b,
             conv_ln_g=conv_ln_g, conv_ln_b=conv_ln_b, w_conv_out=w_conv_out, gmlp_ln_g=gmlp_ln_g,
             gmlp_ln_b=gmlp_ln_b, gmlp_ws=gmlp_ws, gmlp_bs=gmlp_bs, w_gmlp_out=w_gmlp_out,
             pool_w=pool_w, pool_scale=pool_scale, w_pool_out=w_pool_out, w_out=w_out,
             ffn_up=ffn_up, ffn_dw=ffn_dw, ffn_dw_b=ffn_dw_b, ffn_down=ffn_down)
    n_prompt, prompt_len, _ = x_prompt.shape
    n_sample, sample_len, _ = x_sample.shape
    past_len = cache_ckv.shape[2]
    ctx_row = n_sample

    cvec = jnp.zeros((SUBLANES, D_MODEL), F32).at[0:n_sample].set(c).at[ctx_row].set(c_ctx)
    mod = _mod_call(cvec, ada_w, ada_b)

    rope_tabs = _rope_tables(sample_len)
    xp = x_prompt.reshape(n_prompt * prompt_len, D_MODEL)
    xs = x_sample.reshape(n_sample * sample_len, D_MODEL)
    ckv_list, kpe_list = [], []
    stacked, e_kpe = _prep_weights(W)
    mod3 = mod.reshape(DEPTH * SUBLANES, 1, 6 * D_MODEL)
    for l in range(DEPTH):
        lw = _LayerWeights(stacked, e_kpe, l)
        xp, ckv_l, kpe_l = _trunk_layer(xp, prompt_len, mod3, lambda r, l=l: l * SUBLANES + ctx_row, lw, None, None)
        ckv_list.append(ckv_l.reshape(n_prompt, prompt_len, KV_LORA))
        kpe_list.append(kpe_l.reshape(n_prompt, prompt_len, QK_ROPE))
        kpe_pad = jnp.pad(cache_kpe[:, l].reshape(n_sample * past_len, QK_ROPE),
                          ((0, 0), (0, LANES - QK_ROPE)))
        ctx = _ctxkv_call(cache_ckv[:, l].reshape(n_sample * past_len, KV_LORA), kpe_pad, lw)
        xs, _, _ = _trunk_layer(xs, sample_len, mod3, lambda r, l=l: l * SUBLANES + r // sample_len,
                                lw, rope_tabs, ctx)
    new_ckv = jnp.stack(ckv_list, axis=1)
    new_kpe = jnp.stack(kpe_list, axis=1)
    return (xp.reshape(n_prompt, prompt_len, D_MODEL), xs.reshape(n_sample, sample_len, D_MODEL),
            new_ckv, new_kpe)
```

```python
import functools

import jax
import jax.numpy as jnp
import numpy as np
from jax import lax
from jax.experimental import pallas as pl
from jax.experimental.pallas import tpu as pltpu

D_MODEL = 1024
DEPTH = 2
GRID_W = 64
MLA_HEADS = 8
Q_LORA = 384
KV_LORA = 256
QK_NOPE = 64
QK_ROPE = 32
V_HEAD = 64
MLA_W = MLA_HEADS * V_HEAD
SOFTMAX_SCALE = (QK_NOPE + QK_ROPE) ** -0.5
ROPE_BASE = 10000.0
CONV_W = 256
CONV_K = 31
GMLP_W = 256
GMLP_GROUPS = 4
GMLP_CHUNK = 128
POOL_W = 256
POOL_GROUPS = 4
N_BRANCH = 4
D_FF = 2816
FFN_K = 3
NORM_EPS = 1e-6
OFF_KV = Q_LORA
OFF_KPE = OFF_KV + KV_LORA
OFF_CONV = OFF_KPE + QK_ROPE
OFF_GMLP = OFF_CONV + 2 * CONV_W
OFF_POOL = OFF_GMLP + 2 * GMLP_W
OFF_GATE = OFF_POOL + POOL_W

LANES = 128
SUBLANES = 8
VMEM_LIMIT_BYTES = 56 * 1024 * 1024

S_Q = 0
S_KV = S_Q + Q_LORA
S_CONV = S_KV + KV_LORA
S_GMLP = S_CONV + 2 * CONV_W
S_POOL = S_GMLP + 2 * GMLP_W
S_KPE = S_POOL + POOL_W
S_KPEP = S_KPE + LANES
S_TOTAL = S_KPEP + LANES

HEAD_SLOT = LANES
QK_CAT = MLA_HEADS * HEAD_SLOT

ROW_TILE = 512
MERGE_ROWS = 256
FFN_TILE = 1024
FFN_CHUNK = 256
FFN_NCHUNK = D_FF // FFN_CHUNK
SEQ_ROWS = 64
PAD_ROWS = 16
ATTN_Q_TILE = 512
ATTN_GROUP = 8

VEC_FIELDS = (("pre_mix_g", D_MODEL), ("post_mix_g", D_MODEL), ("pre_ffn_g", D_MODEL), ("post_ffn_g", D_MODEL),
              ("b_gate", N_BRANCH * D_MODEL), ("q_norm_g", Q_LORA), ("kv_norm_g", KV_LORA),
              ("conv_dw_b", CONV_W), ("conv_ln_g", CONV_W), ("conv_ln_b", CONV_W),
              ("gmlp_ln_g", GMLP_W), ("gmlp_ln_b", GMLP_W), ("pool_scale", POOL_W), ("ffn_dw_b", 2 * D_FF))
VEC_OFF = {}
for _name, _width in VEC_FIELDS:
    VEC_OFF[_name] = (sum(w for _, w in VEC_FIELDS[:len(VEC_OFF)]), _width)
VEC_TOTAL = sum(w for _, w in VEC_FIELDS)

BF16 = jnp.bfloat16
F32 = jnp.float32
INV_SQRT2 = 0.7071067811865476
LOG2E = 1.4426950408889634


def _dot(a, b):
    return jnp.dot(a, b, preferred_element_type=F32)


def _dot_nt(a, b):
    return lax.dot_general(a, b, (((1,), (1,)), ((), ())), preferred_element_type=F32)


def _rms(x, g):
    return x * lax.rsqrt(jnp.mean(x * x, axis=-1, keepdims=True) + NORM_EPS) * g


def _layernorm(x, g, b):
    mu = jnp.mean(x, axis=-1, keepdims=True)
    xc = x - mu
    var = jnp.mean(xc * xc, axis=-1, keepdims=True)
    return xc * lax.rsqrt(var + NORM_EPS) * g + b


def _sigmoid(x):
    return 1.0 / (1.0 + jnp.exp(-x))


def _vec(vec_ref, name, lo=0, width=None):
    start, full = VEC_OFF[name]
    return vec_ref[:, start + lo:start + lo + (full if width is None else width)]


def _const_spec(shape):
    nd = len(shape)
    return pl.BlockSpec(shape, lambda *_: (0,) * nd, pipeline_mode=pl.Buffered(1))


def _layer_spec(stacked, layer):
    tail = stacked.shape[1:]
    return pl.BlockSpec((None,) + tail, lambda *_: (layer,) + (0,) * len(tail), pipeline_mode=pl.Buffered(1))


def _params(n_axes, flags=None):
    return pltpu.CompilerParams(dimension_semantics=("arbitrary",) * n_axes,
                                vmem_limit_bytes=VMEM_LIMIT_BYTES, flags=flags)


def _mod_kernel(cv_ref, w_ref, b_ref, o_ref):
    cv = cv_ref[...]
    s = (cv * _sigmoid(cv)).astype(BF16)
    o_ref[0] = _dot(s, w_ref[0].astype(BF16)) + b_ref[0]


def _mod_call(cvec, ada_w, ada_b):
    tn = 3072
    n = 6 * D_MODEL
    return pl.pallas_call(
        _mod_kernel,
        grid=(DEPTH, n // tn),
        in_specs=[pl.BlockSpec((SUBLANES, D_MODEL), lambda l, j: (0, 0)),
                  pl.BlockSpec((1, D_MODEL, tn), lambda l, j: (l, 0, j)),
                  pl.BlockSpec((1, 1, tn), lambda l, j: (l, 0, j))],
        out_specs=pl.BlockSpec((1, SUBLANES, tn), lambda l, j: (l, 0, j)),
        out_shape=jax.ShapeDtypeStruct((DEPTH, SUBLANES, n), F32),
        compiler_params=_params(2),
        name="mod",
    )(cvec, ada_w, ada_b.reshape(DEPTH, 1, n))


def _rowproj_kernel(rope, seq_len, x_ref, mod_ref, e_ref, vec_ref, ws_ref, wq_ref, wkv_ref, gws_ref, gbm_ref,
                    cw_ref, pw_ref, *rest):
    rest = list(rest)
    if rope:
        ccat_ref, scat_ref, ckpe_ref, skpe_ref = rest[:4]
        rest = rest[4:]
    apad_ref, ppad_ref, pooled_ref = rest[-3:]
    rest = rest[:-3]
    h_ref, q_ref = rest[:2]
    rest = rest[2:]
    if rope:
        qp_ref = rest.pop(0)
    k_ref, v_ref = rest[:2]
    rest = rest[2:]
    if not rope:
        ckv_ref, kpe_ref = rest[:2]
        rest = rest[2:]
    fb_ref, fd_ref, fc_ref = rest
    x = x_ref[...]
    tm = x.shape[0]
    mod = mod_ref[...]
    sh1 = mod[:, 0:D_MODEL]
    sc1 = mod[:, D_MODEL:2 * D_MODEL]
    h = _rms(x, _vec(vec_ref, "pre_mix_g")) * (1.0 + sc1) + sh1
    hb = h.astype(BF16)
    h_ref[...] = hb

    ci = _dot(hb, ws_ref[:, S_CONV:S_GMLP])
    pin = _dot(hb, ws_ref[:, S_POOL:S_KPE])
    a = ci[:, 0:CONV_W] * _sigmoid(ci[:, CONV_W:2 * CONV_W])
    _seq_branches(seq_len, tm // seq_len, a, pin, cw_ref, vec_ref, fb_ref, apad_ref, ppad_ref, pooled_ref)
    qa = _dot(hb, ws_ref[:, S_Q:S_KV])
    kvc = _dot(hb, ws_ref[:, S_KV:S_CONV])
    gm = _dot(hb, ws_ref[:, S_GMLP:S_POOL])
    kpa = _dot(hb, ws_ref[:, S_KPE:S_KPEP])
    if rope:
        kpb = _dot(hb, ws_ref[:, S_KPEP:S_TOTAL])

    qn = _rms(qa, _vec(vec_ref, "q_norm_g")).astype(BF16)
    ckv = _rms(kvc, _vec(vec_ref, "kv_norm_g"))
    if not rope:
        ckv_ref[...] = ckv
        kpe_ref[...] = kpa[:, 0:QK_ROPE]
    kpr = kpa * ckpe_ref[...] + kpb * skpe_ref[...] if rope else kpa
    qq = _dot(qn, wq_ref[...]) * (SOFTMAX_SCALE * LOG2E)
    kvb = _dot(ckv.astype(BF16), wkv_ref[...])
    kpx = _dot(kpr.astype(BF16), e_ref[...])
    if rope:
        ccat = ccat_ref[...]
        scat = scat_ref[...]
        for hd in range(MLA_HEADS):
            lo = hd * HEAD_SLOT
            qpl = qq[:, lo:lo + HEAD_SLOT]
            qpr = qq[:, QK_CAT + lo:QK_CAT + lo + HEAD_SLOT]
            q_ref[:, lo:lo + HEAD_SLOT] = (qpl * ccat + qpr * scat).astype(BF16)
            qp_ref[:, lo:lo + HEAD_SLOT] = qpl.astype(BF16)
    else:
        q_ref[...] = qq.astype(BF16)
    k_ref[...] = (kvb[:, 0:QK_CAT] + kpx).astype(BF16)
    v_ref[...] = kvb[:, QK_CAT:QK_CAT + MLA_W].astype(BF16)

    uv = 0.5 * gm * (1.0 + lax.erf(gm * INV_SQRT2))
    u = uv[:, 0:GMLP_W]
    v = _layernorm(uv[:, GMLP_W:2 * GMLP_W], _vec(vec_ref, "gmlp_ln_g"), _vec(vec_ref, "gmlp_ln_b"))
    grp = lax.broadcasted_iota(jnp.int32, (GMLP_CHUNK, GMLP_W), 1) // (GMLP_W // GMLP_GROUPS)
    gws = gws_ref[...]
    gbm = gbm_ref[...]
    for n in range(tm // GMLP_CHUNK):
        r0 = n * GMLP_CHUNK
        vc = v[r0:r0 + GMLP_CHUNK]
        vs = jnp.concatenate([jnp.where(grp == g, vc, 0.0) for g in range(GMLP_GROUPS)], axis=0)
        sv = _dot(gws, vs.astype(BF16)) + gbm
        fc_ref[r0:r0 + GMLP_CHUNK, :] = (u[r0:r0 + GMLP_CHUNK] * sv).astype(BF16)
    fd_ref[...] = (_dot(pooled_ref[...].astype(BF16), pw_ref[...]) * _vec(vec_ref, "pool_scale")).astype(BF16)


def _rowproj_call(x2d, mod3, seq_len, mod_row_of_row, lw, rope_tabs):
    rows = x2d.shape[0]
    tm = max(ROW_TILE, seq_len)
    assert tm % seq_len == 0 and rows % tm == 0
    n_seq = tm // seq_len
    rope = rope_tabs is not None
    names = ["vecs", "w_small", "wq_rope" if rope else "wq", "wkv", "gmlp_wcat", "gmlp_bmat", "conv_dw", "pool_bd"]
    row_spec = lambda w: pl.BlockSpec((tm, w), lambda i: (i, 0))
    in_specs = [row_spec(D_MODEL),
                pl.BlockSpec((None, 1, 6 * D_MODEL), lambda i: (mod_row_of_row(i * tm), 0, 0)),
                _const_spec((LANES, QK_CAT))] + [_layer_spec(lw.stacked[n], lw.layer) for n in names]
    args = [x2d, mod3, lw.e_kpe] + [lw.stacked[n] for n in names]
    outs = [("h", D_MODEL, BF16), ("q", QK_CAT, BF16)]
    if rope:
        tiles_per_seq = seq_len // tm
        in_specs += [pl.BlockSpec((tm, LANES), lambda i: (i % tiles_per_seq, 0))] * 4
        args += list(rope_tabs)
        outs.append(("qp", QK_CAT, BF16))
    outs += [("k", QK_CAT, BF16), ("v", MLA_W, BF16)]
    if not rope:
        outs += [("ckv", KV_LORA, F32), ("kpe", QK_ROPE, F32)]
    outs += [("fb", CONV_W, BF16), ("fd", POOL_W, BF16), ("fc", GMLP_W, BF16)]
    span = seq_len + 2 * PAD_ROWS
    res = pl.pallas_call(
        functools.partial(_rowproj_kernel, rope, seq_len),
        grid=(rows // tm,),
        in_specs=in_specs,
        out_specs=[row_spec(w) for _, w, _ in outs],
        out_shape=[jax.ShapeDtypeStruct((rows, w), dt) for _, w, dt in outs],
        scratch_shapes=[pltpu.VMEM((n_seq * span, CONV_W), F32), pltpu.VMEM((n_seq * span, POOL_W), F32),
                        pltpu.VMEM((tm, POOL_W), F32)],
        compiler_params=_params(1),
        name="rowproj_rope" if rope else "rowproj",
    )(*args)
    return {name: r for (name, _, _), r in zip(outs, res)}


def _seq_branches(seq_len, n_seq, a, pin, cw_ref, vec_ref, fb_ref, apad_ref, ppad_ref, pooled_ref):
    L = seq_len
    span = L + 2 * PAD_ROWS
    zeros = jnp.zeros((PAD_ROWS, CONV_W), F32)
    for s in range(n_seq):
        for pad_ref, src in ((apad_ref, a), (ppad_ref, pin)):
            pad_ref[s * span:s * span + PAD_ROWS, :] = zeros
            pad_ref[s * span + L + PAD_ROWS:(s + 1) * span, :] = zeros
            pad_ref[s * span + PAD_ROWS:s * span + PAD_ROWS + L, :] = src[s * L:(s + 1) * L]
    cb = _vec(vec_ref, "conv_dw_b")
    clng = _vec(vec_ref, "conv_ln_g")
    clnb = _vec(vec_ref, "conv_ln_b")
    half_conv = CONV_K // 2
    lane = lax.broadcasted_iota(jnp.int32, (SEQ_ROWS, POOL_W), 1)
    grp_w = POOL_W // POOL_GROUPS
    half = jnp.where(lane < grp_w, 1, jnp.where(lane < 2 * grp_w, 2, jnp.where(lane < 3 * grp_w, 4, 8)))
    win = SEQ_ROWS + 2 * PAD_ROWS

    def shifted_windows(pad_ref, p0):
        w = pad_ref[p0:p0 + win, :]
        return [w] + [pltpu.roll(w, win - b, 0) for b in range(1, SUBLANES)]

    def tap_of(xs, off):
        a0 = off // SUBLANES * SUBLANES
        return xs[off % SUBLANES][a0:a0 + SEQ_ROWS]

    for s in range(n_seq):
        for r0 in range(0, L, SEQ_ROWS):
            out_rows = slice(s * L + r0, s * L + r0 + SEQ_ROWS)
            xs = shifted_windows(apad_ref, s * span + r0)
            acc = jnp.zeros((SEQ_ROWS, CONV_W), F32) + cb
            for k in range(CONV_K):
                acc = acc + tap_of(xs, PAD_ROWS + k - half_conv) * cw_ref[pl.ds(k, 1), :]
            y = _layernorm(acc, clng, clnb)
            fb_ref[out_rows, :] = (y * _sigmoid(y)).astype(BF16)

            ps = shifted_windows(ppad_ref, s * span + r0)
            tap = lambda j: tap_of(ps, PAD_ROWS + j)
            x0 = tap(0)
            s2 = tap(-1) + x0
            s4 = s2 + tap(-2) + tap(1)
            s8 = s4 + tap(-4) + tap(-3) + tap(2) + tap(3)
            s16 = s8 + tap(-8) + tap(-7) + tap(-6) + tap(-5) + tap(4) + tap(5) + tap(6) + tap(7)
            ssum = jnp.where(lane < grp_w, s2, jnp.where(lane < 2 * grp_w, s4, jnp.where(lane < 3 * grp_w, s8, s16)))
            t = r0 + lax.broadcasted_iota(jnp.int32, (SEQ_ROWS, POOL_W), 0)
            cnt = jnp.minimum(t + half, L) - jnp.maximum(t - half, 0)
            pooled_ref[out_rows, :] = ssum / cnt.astype(F32) - x0


def _ctxkv_kernel(ckv_ref, kpe_ref, wkv_ref, e_ref, k_ref, v_ref):
    kvb = _dot(ckv_ref[...].astype(BF16), wkv_ref[...])
    k_ref[...] = (kvb[:, 0:QK_CAT] + _dot(kpe_ref[...].astype(BF16), e_ref[...])).astype(BF16)
    v_ref[...] = kvb[:, QK_CAT:QK_CAT + MLA_W].astype(BF16)


def _ctxkv_call(ckv2d, kpe2d, lw):
    rows = ckv2d.shape[0]
    return pl.pallas_call(
        _ctxkv_kernel,
        grid=(1,),
        in_specs=[pl.BlockSpec((rows, KV_LORA), lambda i: (0, 0)),
                  pl.BlockSpec((rows, LANES), lambda i: (0, 0)),
                  _layer_spec(lw.stacked["wkv"], lw.layer), _const_spec((LANES, QK_CAT))],
        out_specs=[pl.BlockSpec((rows, QK_CAT), lambda i: (0, 0)), pl.BlockSpec((rows, MLA_W), lambda i: (0, 0))],
        out_shape=[jax.ShapeDtypeStruct((rows, QK_CAT), BF16), jax.ShapeDtypeStruct((rows, MLA_W), BF16)],
        compiler_params=_params(1),
        name="ctxkv",
    )(ckv2d, kpe2d, lw.stacked["wkv"], lw.e_kpe)


def _attn_kernel(nseg, group, q_rows, k_rows, *refs):
    q_refs = refs[0:nseg]
    k_refs = refs[nseg:2 * nseg]
    v_refs = refs[2 * nseg:3 * nseg]
    o_ref = refs[3 * nseg]
    out_lane = lax.broadcasted_iota(jnp.int32, (q_rows, LANES), 1)
    for g in range(group):
        qr = slice(g * q_rows, (g + 1) * q_rows)
        krs = [slice(g * n, (g + 1) * n) for n in k_rows]
        scores = [[_dot_nt(q_refs[s][qr, hd * HEAD_SLOT:(hd + 1) * HEAD_SLOT],
                           k_refs[s][krs[s], hd * HEAD_SLOT:(hd + 1) * HEAD_SLOT]) for s in range(nseg)]
                  for hd in range(MLA_HEADS)]
        probs = []
        for hd in range(MLA_HEADS):
            m = functools.reduce(jnp.maximum, [jnp.max(sc, axis=-1, keepdims=True) for sc in scores[hd]])
            probs.append([jnp.exp2(sc - m).astype(BF16) for sc in scores[hd]])
        for pair in range(MLA_HEADS // 2):
            plo = pair * LANES
            outs = []
            for hh in range(2):
                o = None
                for s in range(nseg):
                    vp = v_refs[s][krs[s], plo:plo + LANES]
                    lane = lax.broadcasted_iota(jnp.int32, vp.shape, 1)
                    own = (lane < V_HEAD) if hh == 0 else (lane >= V_HEAD)
                    t = _dot(probs[2 * pair + hh][s], jnp.where(own, vp, jnp.ones_like(vp)))
                    o = t if o is None else o + t
                outs.append(o * (1.0 / pltpu.roll(o, V_HEAD, 1)))
            o_ref[qr, plo:plo + LANES] = jnp.where(out_lane < V_HEAD, outs[0], outs[1]).astype(BF16)


def _attn_call(qs, ks, vs, seq_len):
    nseg = len(qs)
    rows = qs[0].shape[0]
    nb = rows // seq_len
    tq = min(ATTN_Q_TILE, seq_len)
    nq = seq_len // tq
    group = ATTN_GROUP if nq == 1 else 1
    k_rows = [k.shape[0] // nb for k in ks]
    q_spec = pl.BlockSpec((group * tq, QK_CAT), lambda b, j: (b * nq + j, 0))
    k_specs = [pl.BlockSpec((group * n, QK_CAT), lambda b, j: (b, 0)) for n in k_rows]
    v_specs = [pl.BlockSpec((group * n, MLA_W), lambda b, j: (b, 0)) for n in k_rows]
    return pl.pallas_call(
        functools.partial(_attn_kernel, nseg, group, tq, k_rows),
        grid=(nb // group, nq),
        in_specs=[q_spec] * nseg + k_specs + v_specs,
        out_specs=pl.BlockSpec((group * tq, MLA_W), lambda b, j: (b * nq + j, 0)),
        out_shape=jax.ShapeDtypeStruct((rows, MLA_W), BF16),
        compiler_params=_params(2),
        name="attn%d" % nseg,
    )(*qs, *ks, *vs)


def _merge_kernel(x_ref, mod_ref, h_ref, fa_ref, fb_ref, fc_ref, fd_ref, vec_ref, wg_ref,
                  wa_ref, wb_ref, wc_ref, wd_ref, wout_ref, x1_ref, h2_ref):
    tm = h_ref.shape[0]
    row_blocks = [slice(r0, r0 + MERGE_ROWS) for r0 in range(0, tm, MERGE_ROWS)]
    mod = mod_ref[...]
    g1 = mod[:, 2 * D_MODEL:3 * D_MODEL]
    sh2 = mod[:, 3 * D_MODEL:4 * D_MODEL]
    sc2 = mod[:, 4 * D_MODEL:5 * D_MODEL]
    merged = []
    for rs in row_blocks:
        hb = h_ref[rs, :]
        acc = None
        for k, (f_ref, w_ref) in enumerate(((fa_ref, wa_ref), (fb_ref, wb_ref), (fc_ref, wc_ref), (fd_ref, wd_ref))):
            cols = slice(k * D_MODEL, (k + 1) * D_MODEL)
            gate = _sigmoid(_dot(hb, wg_ref[:, cols]) + _vec(vec_ref, "b_gate", k * D_MODEL, D_MODEL))
            t = gate * _dot(f_ref[rs, :], w_ref[...])
            acc = t if acc is None else acc + t
        merged.append(acc.astype(BF16))
    mixes = [_dot(m, wout_ref[...]) for m in merged]
    for rs, mix in zip(row_blocks, mixes):
        x1 = x_ref[rs, :] + g1 * _rms(mix, _vec(vec_ref, "post_mix_g"))
        x1_ref[rs, :] = x1
        h2_ref[rs, :] = (_rms(x1, _vec(vec_ref, "pre_ffn_g")) * (1.0 + sc2) + sh2).astype(BF16)


def _merge_call(x2d, mod3, mod_row_of_row, h, fa, fb, fc, fd, lw):
    rows = x2d.shape[0]
    tm = ROW_TILE
    names = ["vecs", "w_gate", "w_o_mla", "w_conv_out", "w_gmlp_out", "w_pool_out", "w_out"]
    row_spec = lambda w: pl.BlockSpec((tm, w), lambda i: (i, 0))
    return pl.pallas_call(
        _merge_kernel,
        grid=(rows // tm,),
        in_specs=[row_spec(D_MODEL),
                  pl.BlockSpec((None, 1, 6 * D_MODEL), lambda i: (mod_row_of_row(i * tm), 0, 0)),
                  row_spec(D_MODEL), row_spec(MLA_W), row_spec(CONV_W), row_spec(GMLP_W), row_spec(POOL_W)]
                 + [_layer_spec(lw.stacked[n], lw.layer) for n in names],
        out_specs=[row_spec(D_MODEL), row_spec(D_MODEL)],
        out_shape=[jax.ShapeDtypeStruct((rows, D_MODEL), F32), jax.ShapeDtypeStruct((rows, D_MODEL), BF16)],
        compiler_params=_params(1),
        name="merge",
    )(x2d, mod3, h, fa, fb, fc, fd, *[lw.stacked[n] for n in names])


def _ffn_kernel(seq_len, h2_ref, x1_ref, mod_ref, vec_ref, wup_ref, dw_ref, wdn_ref, o_ref, act_ref):
    tm = h2_ref.shape[0]
    assert seq_len & (seq_len - 1) == 0
    pos = lax.broadcasted_iota(jnp.int32, (tm, FFN_CHUNK), 0) & (seq_len - 1)
    first = pos == 0
    last = pos == seq_len - 1

    def conv_cols(c0):
        cols = slice(c0, c0 + FFN_CHUNK)
        up = _dot(h2_ref[...], wup_ref[:, cols])
        dw = dw_ref[:, cols]
        prev = jnp.where(first, 0.0, pltpu.roll(up, 1, 0))
        nxt = jnp.where(last, 0.0, pltpu.roll(up, tm - 1, 0))
        return prev * dw[0:1, :] + up * dw[1:2, :] + nxt * dw[2:3, :] + _vec(vec_ref, "ffn_dw_b", c0, FFN_CHUNK)

    for c in range(FFN_NCHUNK):
        c0 = c * FFN_CHUNK
        g = conv_cols(c0)
        val = conv_cols(D_FF + c0)
        act_ref[:, c0:c0 + FFN_CHUNK] = (g * _sigmoid(g) * val).astype(BF16)
    f = _dot(act_ref[...], wdn_ref[...])
    g2 = mod_ref[...][:, 5 * D_MODEL:6 * D_MODEL]
    o_ref[...] = x1_ref[...] + g2 * _rms(f, _vec(vec_ref, "post_ffn_g"))


def _ffn_call(h2, x1, mod3, seq_len, mod_row_of_row, lw):
    rows = h2.shape[0]
    tm = max(FFN_TILE, seq_len)
    names = ["vecs", "ffn_up", "ffn_dw", "ffn_down"]
    row_spec = pl.BlockSpec((tm, D_MODEL), lambda i: (i, 0))
    return pl.pallas_call(
        functools.partial(_ffn_kernel, seq_len),
        grid=(rows // tm,),
        in_specs=[row_spec, row_spec,
                  pl.BlockSpec((None, 1, 6 * D_MODEL), lambda i: (mod_row_of_row(i * tm), 0, 0))]
                 + [_layer_spec(lw.stacked[n], lw.layer) for n in names],
        out_specs=row_spec,
        out_shape=jax.ShapeDtypeStruct((rows, D_MODEL), F32),
        scratch_shapes=[pltpu.VMEM((tm, D_FF), BF16)],
        compiler_params=_params(1),
        name="ffn",
    )(h2, x1, mod3, *[lw.stacked[n] for n in names])


def _rope_tables(seq_len):
    quarter = QK_ROPE // 4
    inv = ROPE_BASE ** (-jnp.arange(quarter, dtype=F32) / quarter)
    t = jnp.arange(seq_len)
    row = (t // GRID_W).astype(F32)[:, None] * inv[None, :]
    col = (t % GRID_W).astype(F32)[:, None] * inv[None, :]
    ang = jnp.concatenate([row, row, col, col], axis=1)
    cos, sin = jnp.cos(ang), jnp.sin(ang)
    ones = jnp.ones((seq_len, QK_NOPE), F32)
    zeros_n = jnp.zeros((seq_len, QK_NOPE), F32)
    zeros_t = jnp.zeros((seq_len, HEAD_SLOT - QK_NOPE - QK_ROPE), F32)
    ccat = jnp.concatenate([ones, cos, zeros_t], axis=1)
    scat = jnp.concatenate([zeros_n, sin, zeros_t], axis=1)
    zeros_k = jnp.zeros((seq_len, LANES - QK_ROPE), F32)
    ckpe = jnp.concatenate([cos, zeros_k], axis=1)
    skpe = jnp.concatenate([sin, zeros_k], axis=1)
    return ccat, scat, ckpe, skpe


class _LayerWeights:
    def __init__(self, stacked, e_kpe, layer):
        self.stacked = stacked
        self.e_kpe = e_kpe
        self.layer = layer


PREP_COLS = 512
N_GATE_STEPS = N_BRANCH * D_MODEL // PREP_COLS


def _split_w_in_kernel(wt_ref, small_ref, gate_ref):
    j = pl.program_id(1)

    @pl.when(j < N_GATE_STEPS)
    def _():
        r0 = pl.multiple_of(OFF_GATE + j * PREP_COLS, SUBLANES)
        gate_ref[...] = wt_ref[pl.ds(r0, PREP_COLS), :].T.astype(BF16)

    @pl.when(j == N_GATE_STEPS)
    def _():
        def put(c0, rows_t):
            small_ref[:, c0:c0 + rows_t.shape[0]] = rows_t.T.astype(BF16)

        for r0 in range(0, OFF_KPE, LANES):
            put(S_Q + r0, wt_ref[r0:r0 + LANES, :])
        for r0 in range(OFF_CONV, OFF_GATE, LANES):
            put(S_CONV + r0 - OFF_CONV, wt_ref[r0:r0 + LANES, :])
        kpe = wt_ref[OFF_KPE:OFF_CONV, :]
        quarter = QK_ROPE // 4
        rot = []
        for b0 in range(0, QK_ROPE, 2 * quarter):
            rot += [-kpe[b0 + quarter:b0 + 2 * quarter], kpe[b0:b0 + quarter]]
        zeros = jnp.zeros((LANES - QK_ROPE, D_MODEL), F32)
        put(S_KPE, jnp.concatenate([kpe, zeros], axis=0))
        put(S_KPEP, jnp.concatenate(rot + [zeros], axis=0))


def _split_w_in_call(w_in):
    depth, d_model, cols = w_in.shape
    wt = jnp.swapaxes(w_in, 1, 2)
    last = N_GATE_STEPS - 1
    return pl.pallas_call(
        _split_w_in_kernel,
        grid=(depth, N_GATE_STEPS + 1),
        in_specs=[pl.BlockSpec((None, cols, d_model), lambda l, j: (l, 0, 0), pipeline_mode=pl.Buffered(1))],
        out_specs=[pl.BlockSpec((None, d_model, S_TOTAL), lambda l, j: (l, 0, 0)),
                   pl.BlockSpec((None, d_model, PREP_COLS), lambda l, j: (l, 0, jnp.minimum(j, last)))],
        out_shape=[jax.ShapeDtypeStruct((depth, d_model, S_TOTAL), BF16),
                   jax.ShapeDtypeStruct((depth, d_model, N_BRANCH * D_MODEL), BF16)],
        compiler_params=_params(2),
        name="split_w_in",
    )(wt)


def _prep_weights(W):
    depth = W["w_in"].shape[0]
    w_small, w_gate = _split_w_in_call(W["w_in"])

    quarter = QK_ROPE // 4
    q_head = QK_NOPE + QK_ROPE
    place_q = np.zeros((MLA_HEADS * q_head, 2 * QK_CAT), np.float32)
    place_kv = np.zeros((MLA_HEADS * (QK_NOPE + V_HEAD), QK_CAT + MLA_W), np.float32)
    e = np.zeros((LANES, QK_CAT), np.float32)
    for hd in range(MLA_HEADS):
        for j in range(q_head):
            place_q[hd * q_head + j, hd * HEAD_SLOT + j] = 1.0
        for d in range(QK_ROPE):
            first_half = d % (2 * quarter) < quarter
            src = d + quarter if first_half else d - quarter
            place_q[hd * q_head + QK_NOPE + src, QK_CAT + hd * HEAD_SLOT + QK_NOPE + d] = -1.0 if first_half else 1.0
            e[d, hd * HEAD_SLOT + QK_NOPE + d] = 1.0
        for j in range(QK_NOPE):
            place_kv[hd * (QK_NOPE + V_HEAD) + j, hd * HEAD_SLOT + j] = 1.0
        for j in range(V_HEAD):
            place_kv[hd * (QK_NOPE + V_HEAD) + QK_NOPE + j, QK_CAT + hd * V_HEAD + j] = 1.0
    place = lambda w, p: jnp.einsum("lqk,kn->lqn", w.astype(BF16), jnp.asarray(p, BF16), preferred_element_type=BF16)
    wq_rope = place(W["w_q_b"], place_q)
    wq_cat = wq_rope[:, :, :QK_CAT]
    wkv_cat = place(W["w_kv_b"], place_kv)

    eye = jnp.eye(POOL_GROUPS, dtype=F32)
    pool_bd = (eye[None, :, None, :, None] * W["pool_w"][:, :, :, None, :]).reshape(depth, POOL_W, POOL_W)

    vecs = jnp.concatenate([W[name] for name, _ in VEC_FIELDS], axis=1).reshape(depth, 1, VEC_TOTAL)
    stacked = {
        "vecs": vecs, "w_small": w_small, "w_gate": w_gate,
        "wq": wq_cat, "wq_rope": wq_rope, "wkv": wkv_cat,
        "w_o_mla": W["w_o_mla"].astype(BF16),
        "conv_dw": W["conv_dw"],
        "w_conv_out": W["w_conv_out"].astype(BF16),
        "gmlp_wcat": W["gmlp_ws"].transpose(0, 2, 1, 3).reshape(depth, GMLP_CHUNK, GMLP_GROUPS * GMLP_CHUNK).astype(BF16),
        "gmlp_bmat": jnp.repeat(W["gmlp_bs"].transpose(0, 2, 1), GMLP_W // GMLP_GROUPS, axis=2),
        "w_gmlp_out": W["w_gmlp_out"].astype(BF16),
        "pool_bd": pool_bd.astype(BF16),
        "w_pool_out": W["w_pool_out"].astype(BF16),
        "w_out": W["w_out"].astype(BF16),
        "ffn_up": W["ffn_up"].astype(BF16),
        "ffn_dw": W["ffn_dw"],
        "ffn_down": W["ffn_down"].astype(BF16),
    }
    return stacked, jnp.asarray(e, BF16)


def _trunk_layer(x2d, seq_len, mod3, mod_row_of_row, lw, rope_tabs, ctx):
    p = _rowproj_call(x2d, mod3, seq_len, mod_row_of_row, lw, rope_tabs)
    if ctx is None:
        fa = _attn_call([p["q"]], [p["k"]], [p["v"]], seq_len)
    else:
        k_ctx, v_ctx = ctx
        fa = _attn_call([p["q"], p["qp"]], [p["k"], k_ctx], [p["v"], v_ctx], seq_len)
    x1, h2 = _merge_call(x2d, mod3, mod_row_of_row, p["h"], fa, p["fb"], p["fc"], p["fd"], lw)
    x2 = _ffn_call(h2, x1, mod3, seq_len, mod_row_of_row, lw)
    return x2, p.get("ckv"), p.get("kpe")


def kernel(x_prompt, x_sample, cache_ckv, cache_kpe, c, c_ctx, ada_w, ada_b, pre_mix_g, post_mix_g, pre_ffn_g, post_ffn_g, w_in, b_gate, q_norm_g, w_q_b, kv_norm_g, w_kv_b, w_o_mla, conv_dw, conv_dw_b, conv_ln_g, conv_ln_b, w_conv_out, gmlp_ln_g, gmlp_ln_b, gmlp_ws, gmlp_bs, w_gmlp_out, pool_w, pool_scale, w_pool_out, w_out, ffn_up, ffn_dw, ffn_dw_b, ffn_down):
    W = dict(pre_mix_g=pre_mix_g, post_mix_g=post_mix_g, pre_ffn_g=pre_ffn_g, post_ffn_g=post_ffn_g,
             w_in=w_in, b_gate=b_gate, q_norm_g=q_norm_g, w_q_b=w_q_b, kv_norm_g=kv_norm_g,
             w_kv_b=w_kv_b, w_o_mla=w_o_mla, conv_dw=conv_dw, conv_dw_b=conv_dw_b,
             conv_ln_g=conv_ln_g, conv_ln_b=conv_ln_b, w_conv_out=w_conv_out, gmlp_ln_g=gmlp_ln_g,
             gmlp_ln_b=gmlp_ln_b, gmlp_ws=gmlp_ws, gmlp_bs=gmlp_bs, w_gmlp_out=w_gmlp_out,
             pool_w=pool_w, pool_scale=pool_scale, w_pool_out=w_pool_out, w_out=w_out,
             ffn_up=ffn_up, ffn_dw=ffn_dw, ffn_dw_b=ffn_dw_b, ffn_down=ffn_down)
    n_prompt, prompt_len, _ = x_prompt.shape
    n_sample, sample_len, _ = x_sample.shape
    past_len = cache_ckv.shape[2]
    ctx_row = n_sample

    cvec = jnp.zeros((SUBLANES, D_MODEL), F32).at[0:n_sample].set(c).at[ctx_row].set(c_ctx)
    mod = _mod_call(cvec, ada_w, ada_b)

    rope_tabs = _rope_tables(sample_len)
    xp = x_prompt.reshape(n_prompt * prompt_len, D_MODEL)
    xs = x_sample.reshape(n_sample * sample_len, D_MODEL)
    ckv_list, kpe_list = [], []
    stacked, e_kpe = _prep_weights(W)
    mod3 = mod.reshape(DEPTH * SUBLANES, 1, 6 * D_MODEL)
    for l in range(DEPTH):
        lw = _LayerWeights(stacked, e_kpe, l)
        xp, ckv_l, kpe_l = _trunk_layer(xp, prompt_len, mod3, lambda r, l=l: l * SUBLANES + ctx_row, lw, None, None)
        ckv_list.append(ckv_l.reshape(n_prompt, prompt_len, KV_LORA))
        kpe_list.append(kpe_l.reshape(n_prompt, prompt_len, QK_ROPE))
        kpe_pad = jnp.pad(cache_kpe[:, l].reshape(n_sample * past_len, QK_ROPE),
                          ((0, 0), (0, LANES - QK_ROPE)))
        ctx = _ctxkv_call(cache_ckv[:, l].reshape(n_sample * past_len, KV_LORA), kpe_pad, lw)
        xs, _, _ = _trunk_layer(xs, sample_len, mod3, lambda r, l=l: l * SUBLANES + r // sample_len,
                                lw, rope_tabs, ctx)
    new_ckv = jnp.stack(ckv_list, axis=1)
    new_kpe = jnp.stack(kpe_list, axis=1)
    return (xp.reshape(n_prompt, prompt_len, D_MODEL), xs.reshape(n_sample, sample_len, D_MODEL),
            new_ckv, new_kpe)
```
